```python
import math
import jax, jax.numpy as jnp
from jax import lax
import numpy as np

D_MODEL = 1024
BATCH = 4
SEQ = 4096
DEPTH = 4

N_MIXERS = 2
CONV_EXPAND = 2
CONV_WIDTH = D_MODEL * CONV_EXPAND
CONV_K = 3
HEAD_DIM = 64
HEADS_PER_GROUP = D_MODEL // HEAD_DIM
DILATED_GROUPS = ((128, 1), (512, 4), (2048, 16))
N_GROUPS = len(DILATED_GROUPS)
ATTN_WIDTH = HEADS_PER_GROUP * HEAD_DIM
QKV_COLS = N_GROUPS * 3 * ATTN_WIDTH
BLOCK = 128
N_BUCKETS = 32
MAX_DISTANCE = 2048
EPS = 1e-6
N_CONV_LAYERS = (DEPTH + 1) // 2
N_ATTN_LAYERS = DEPTH // 2

kernel_name = "hybrid_shortconv_dilated_attn_trunk"


def rms_norm(x, g):
    xf = x.astype(jnp.float32)
    y = xf * lax.rsqrt(jnp.mean(xf * xf, axis=-1, keepdims=True) + EPS)
    return (y * g.astype(jnp.float32)).astype(x.dtype)


def t5_bucket(dist):
    max_exact = N_BUCKETS // 2
    d = jnp.maximum(dist, 1).astype(jnp.float32)
    large = max_exact + (jnp.log(d / max_exact) / math.log(MAX_DISTANCE / max_exact)
                         * (N_BUCKETS - max_exact)).astype(jnp.int32)
    large = jnp.minimum(large, N_BUCKETS - 1)
    return jnp.where(dist < max_exact, dist, large)


def short_conv_mixer(h, w_in, w_conv, w_out):
    S = h.shape[1]
    proj = h @ w_in
    b_gate, c_gate, u, z = jnp.split(proj, 4, axis=-1)
    v = c_gate * u
    vp = jnp.pad(v, ((0, 0), (CONV_K - 1, 0), (0, 0)))
    conv = sum(w_conv[k] * vp[:, k:k + S] for k in range(CONV_K))
    y = b_gate * conv * jax.nn.silu(z)
    return y @ w_out


def dilated_group_attention(q, k, v, bias_tab, window, dilation):
    B, S, H, Dh = q.shape
    L = S // dilation
    nb = -(-L // BLOCK)
    Lp = nb * BLOCK
    span = window // dilation

    def to_sub(t):
        t = t.reshape(B, L, dilation, H, Dh).transpose(0, 2, 3, 1, 4)
        t = jnp.pad(t, ((0, 0), (0, 0), (0, 0), (0, Lp - L), (0, 0)))
        return t.reshape(B, dilation, H, nb, BLOCK, Dh)

    def band(t):
        prev = jnp.pad(t, ((0, 0), (0, 0), (0, 0), (1, 0), (0, 0), (0, 0)))[:, :, :, :-1]
        return jnp.concatenate([prev, t], axis=4)

    qs = to_sub(q)
    kb = band(to_sub(k))
    vb = band(to_sub(v))
    logits = jnp.einsum('brhnqd,brhnkd->brhnqk', qs, kb).astype(jnp.float32) * (HEAD_DIM ** -0.5)

    a = jnp.arange(BLOCK)[:, None]
    kk = jnp.arange(2 * BLOCK)[None, :]
    step = BLOCK + a - kk
    blk = jnp.arange(nb)[:, None, None]
    key_pos = (blk - 1) * BLOCK + kk[None]
    valid = (step >= 0) & (step <= span) & (key_pos >= 0)
    bucket = t5_bucket(jnp.clip(step, 0, span) * dilation)
    bias = bias_tab.astype(jnp.float32)[bucket].transpose(2, 0, 1)[:, None]

    logits = jnp.where(valid, logits + bias, -jnp.inf)
    lse = jax.nn.logsumexp(logits, axis=-1)
    probs = jnp.exp(logits - lse[..., None])
    out = jnp.einsum('brhnqk,brhnkd->brhnqd', probs.astype(v.dtype), vb)

    out = out.reshape(B, dilation, H, Lp, Dh)[:, :, :, :L]
    out = out.transpose(0, 3, 1, 2, 4).reshape(B, S, H, Dh)
    lse = lse.reshape(B, dilation, H, Lp)[:, :, :, :L]
    lse = lse.transpose(0, 3, 1, 2).reshape(B, S, H)
    return out, lse


def dilated_attention_mixer(h, w_in, q_gain, k_gain, w_out, rel_bias):
    B, S, _ = h.shape
    proj = h @ w_in
    qkv = proj[..., :QKV_COLS].reshape(B, S, N_GROUPS, 3, HEADS_PER_GROUP, HEAD_DIM)
    z = proj[..., QKV_COLS:]
    outs, lses = [], []
    for g, (window, dilation) in enumerate(DILATED_GROUPS):
        q = rms_norm(qkv[:, :, g, 0], q_gain[g])
        k = rms_norm(qkv[:, :, g, 1], k_gain[g])
        bias_g = rel_bias[:, g * HEADS_PER_GROUP:(g + 1) * HEADS_PER_GROUP]
        o, l = dilated_group_attention(q, k, qkv[:, :, g, 2], bias_g, window, dilation)
        outs.append(o)
        lses.append(l)
    alpha = jax.nn.softmax(jnp.stack(lses, axis=0), axis=0)
    o = jnp.sum(alpha[..., None] * jnp.stack(outs, axis=0).astype(jnp.float32), axis=0)
    y = o.reshape(B, S, ATTN_WIDTH).astype(h.dtype) * jax.nn.silu(z)
    return y @ w_out


def setup_inputs(seed: int = 0) -> dict:
    key = jax.random.key(seed)
    ks = jax.random.split(key, 12)
    f32 = jnp.float32
    nc, na = N_CONV_LAYERS, N_ATTN_LAYERS
    x = jax.random.normal(ks[0], (BATCH, SEQ, D_MODEL), f32)
    conv_norm = 1.0 + 0.1 * jax.random.normal(ks[1], (nc, D_MODEL), f32)
    conv_w_in = jax.random.normal(ks[2], (nc, D_MODEL, 4 * CONV_WIDTH), f32) * D_MODEL ** -0.5
    conv_w = jax.random.normal(ks[3], (nc, CONV_K, CONV_WIDTH), f32) * CONV_K ** -0.5
    conv_w_out = jax.random.normal(ks[4], (nc, CONV_WIDTH, D_MODEL), f32) * CONV_WIDTH ** -0.5
    attn_norm = 1.0 + 0.1 * jax.random.normal(ks[5], (na, D_MODEL), f32)
    attn_w_in = jax.random.normal(ks[6], (na, D_MODEL, QKV_COLS + ATTN_WIDTH), f32) * D_MODEL ** -0.5
    attn_q_gain = 1.0 + 0.1 * jax.random.normal(ks[7], (na, N_GROUPS, HEAD_DIM), f32)
    attn_k_gain = 1.0 + 0.1 * jax.random.normal(ks[8], (na, N_GROUPS, HEAD_DIM), f32)
    attn_w_out = jax.random.normal(ks[9], (na, ATTN_WIDTH, D_MODEL), f32) * ATTN_WIDTH ** -0.5
    rel_bias = 0.5 * jax.random.normal(ks[10], (N_BUCKETS, N_GROUPS * HEADS_PER_GROUP), f32)
    return {"x": x, "conv_norm": conv_norm, "conv_w_in": conv_w_in, "conv_w": conv_w,
            "conv_w_out": conv_w_out, "attn_norm": attn_norm, "attn_w_in": attn_w_in,
            "attn_q_gain": attn_q_gain, "attn_k_gain": attn_k_gain, "attn_w_out": attn_w_out,
            "rel_bias": rel_bias}


def reference(x, conv_norm, conv_w_in, conv_w, conv_w_out, attn_norm, attn_w_in,
              attn_q_gain, attn_k_gain, attn_w_out, rel_bias):
    for i in range(DEPTH):
        j = i // N_MIXERS
        if i % N_MIXERS == 0:
            h = rms_norm(x, conv_norm[j])
            x = x + short_conv_mixer(h, conv_w_in[j], conv_w[j], conv_w_out[j])
        else:
            h = rms_norm(x, attn_norm[j])
            x = x + dilated_attention_mixer(h, attn_w_in[j], attn_q_gain[j], attn_k_gain[j],
                                            attn_w_out[j], rel_bias)
    return x
```

```python
import functools
import math

import jax
import jax.numpy as jnp
from jax import lax
from jax.experimental import pallas as pl
from jax.experimental.pallas import tpu as pltpu

D_MODEL = 1024
CONV_WIDTH = 2048
CONV_K = 3
HEAD_DIM = 64
N_HEADS = 16
ATTN_WIDTH = N_HEADS * HEAD_DIM
DILATED_GROUPS = ((128, 1), (512, 4), (2048, 16))
N_GROUPS = len(DILATED_GROUPS)
QKV_COLS = N_GROUPS * 3 * ATTN_WIDTH
BLOCK = 128
N_BUCKETS = 32
MAX_DISTANCE = 2048
EPS = 1e-6
MASKED = -1e30

LANES = 128
MXU_COLS = 256
HEADS_PER_TILE = LANES // HEAD_DIM
N_PAIRS = N_HEADS // HEADS_PER_TILE
VMEM_LIMIT = 56 * 1024 * 1024

F32 = jnp.float32
BF16 = jnp.bfloat16


def _resident(shape):
    return pl.BlockSpec(shape, lambda *_: (0,) * len(shape),
                        pipeline_mode=pl.Buffered(1))


def _params():
    return pltpu.CompilerParams(
        dimension_semantics=("arbitrary",), vmem_limit_bytes=VMEM_LIMIT)


def _rms_rows(x, gain):
    ms = jnp.mean(x * x, axis=-1, keepdims=True)
    return x * lax.rsqrt(ms + EPS) * gain


def _conv_layer_kernel(x_ref, g_ref, win_ref, wc_ref, wout_ref, o_ref,
                       h_ref, carry_ref, *, tm, te, tiles_per_seq):
    i = pl.program_id(0)
    x = x_ref[...]
    h_ref[...] = _rms_rows(x, g_ref[...]).astype(BF16)

    @pl.when(i % tiles_per_seq == 0)
    def _():
        carry_ref[...] = jnp.zeros_like(carry_ref)

    rows = lax.broadcasted_iota(jnp.int32, (tm, te), 0)
    for j in range(CONV_WIDTH // te):
        cols = slice(j * te, (j + 1) * te)

        def proj(p):
            w = win_ref[:, p * CONV_WIDTH + j * te:p * CONV_WIDTH + (j + 1) * te]
            return jnp.dot(h_ref[...], w, preferred_element_type=F32)

        v = proj(1) * proj(2)
        prev1 = carry_ref[1:2, cols]
        prev2 = carry_ref[0:1, cols]
        v1 = jnp.where(rows == 0, prev1, pltpu.roll(v, 1, 0))
        v2 = jnp.where(rows == 0, prev2,
                       jnp.where(rows == 1, prev1, pltpu.roll(v, 2, 0)))
        carry_ref[0:2, cols] = v[tm - 2:tm]
        wc = wc_ref[:, cols]
        conv = wc[0:1] * v2 + wc[1:2] * v1 + wc[2:3] * v
        y = proj(0) * conv * jax.nn.silu(proj(3))
        part = jnp.dot(y.astype(BF16), wout_ref[cols, :],
                       preferred_element_type=F32)
        if j == 0:
            o_ref[...] = x + part
        else:
            o_ref[...] += part


def _conv_layer(x2, norm_g, w_in, w_conv, w_out, seq):
    t = x2.shape[0]
    tm, te = 512, 512
    kern = functools.partial(_conv_layer_kernel, tm=tm, te=te,
                             tiles_per_seq=seq // tm)
    return pl.pallas_call(
        kern,
        grid=(t // tm,),
        in_specs=[
            pl.BlockSpec((tm, D_MODEL), lambda i: (i, 0)),
            _resident((1, D_MODEL)),
            _resident((D_MODEL, 4 * CONV_WIDTH)),
            _resident((CONV_K, CONV_WIDTH)),
            _resident((CONV_WIDTH, D_MODEL)),
        ],
        out_specs=pl.BlockSpec((tm, D_MODEL), lambda i: (i, 0)),
        out_shape=jax.ShapeDtypeStruct((t, D_MODEL), F32),
        scratch_shapes=[pltpu.VMEM((tm, D_MODEL), BF16),
                        pltpu.VMEM((8, CONV_WIDTH), F32)],
        compiler_params=_params(),
        name="conv_layer",
    )(x2, norm_g.reshape(1, D_MODEL), w_in.astype(BF16), w_conv,
      w_out.astype(BF16))


def _attn_proj_kernel(x_ref, g_ref, w_ref, qg_ref, kg_ref, bd_ref,
                      qkv_ref, gate_ref, h_ref):
    h_ref[...] = _rms_rows(x_ref[...], g_ref[...]).astype(BF16)
    n_chunks = QKV_COLS // ATTN_WIDTH
    for c in range(n_chunks + 1):
        a = jnp.dot(h_ref[...], w_ref[:, c * ATTN_WIDTH:(c + 1) * ATTN_WIDTH],
                    preferred_element_type=F32)
        if c == n_chunks:
            gate_ref[...] = jax.nn.silu(a).astype(BF16)
            continue
        g, part = divmod(c, 3)
        if part == 2:
            qkv_ref[:, c * ATTN_WIDTH:(c + 1) * ATTN_WIDTH] = a.astype(BF16)
            continue
        gain_ref = qg_ref if part == 0 else kg_ref
        for s in range(ATTN_WIDTH // MXU_COLS):
            sl = slice(s * MXU_COLS, (s + 1) * MXU_COLS)
            a_s = a[:, sl]
            ss = jnp.dot((a_s * a_s).astype(BF16), bd_ref[...],
                         preferred_element_type=F32)
            a_s = a_s * lax.rsqrt(ss * (1.0 / HEAD_DIM) + EPS) * gain_ref[g:g + 1, sl]
            qkv_ref[:, c * ATTN_WIDTH + s * MXU_COLS:
                    c * ATTN_WIDTH + (s + 1) * MXU_COLS] = a_s.astype(BF16)


def _attn_proj(x2, norm_g, w_in, q_gain, k_gain):
    t = x2.shape[0]
    tm = 512
    head_of_lane = jnp.arange(MXU_COLS) // HEAD_DIM
    block_diag = (head_of_lane[:, None] == head_of_lane[None, :]).astype(BF16)
    qg = jnp.tile(q_gain, (1, N_HEADS)) * (HEAD_DIM ** -0.5)
    kg = jnp.tile(k_gain, (1, N_HEADS))
    return pl.pallas_call(
        _attn_proj_kernel,
        grid=(t // tm,),
        in_specs=[
            pl.BlockSpec((tm, D_MODEL), lambda i: (i, 0)),
            _resident((1, D_MODEL)),
            _resident((D_MODEL, QKV_COLS + ATTN_WIDTH)),
            _resident((N_GROUPS, ATTN_WIDTH)),
            _resident((N_GROUPS, ATTN_WIDTH)),
            _resident((MXU_COLS, MXU_COLS)),
        ],
        out_specs=[pl.BlockSpec((tm, QKV_COLS), lambda i: (i, 0)),
                   pl.BlockSpec((tm, ATTN_WIDTH), lambda i: (i, 0))],
        out_shape=[jax.ShapeDtypeStruct((t, QKV_COLS), BF16),
                   jax.ShapeDtypeStruct((t, ATTN_WIDTH), BF16)],
        scratch_shapes=[pltpu.VMEM((tm, D_MODEL), BF16)],
        compiler_params=_params(),
        name="attn_proj",
    )(x2, norm_g.reshape(1, D_MODEL), w_in.astype(BF16), qg, kg, block_diag)


def _attn_group_kernel(q_ref, k_ref, v_ref, bias_ref, o_ref, lse_ref,
                       kprev_ref, vprev_ref, *, nq):
    n = pl.program_id(2)

    @pl.when(n == 0)
    def _():
        kprev_ref[...] = jnp.zeros_like(kprev_ref)
        vprev_ref[...] = jnp.zeros_like(vprev_ref)

    first = jnp.where(n == 0, 1, 0)
    lane = lax.broadcasted_iota(jnp.int32, (BLOCK, LANES), 1)
    low = lane < HEAD_DIM
    lane_row = lax.broadcasted_iota(jnp.int32, (1, LANES), 1)
    low_b = (lane_row < HEAD_DIM).astype(F32).astype(BF16)
    high_b = (lane_row >= HEAD_DIM).astype(F32).astype(BF16)

    for qb in range(nq):
        rows = slice(qb * BLOCK, (qb + 1) * BLOCK)
        lse_tile = jnp.zeros((BLOCK, LANES), F32)
        for pr in range(N_PAIRS):
            cols = slice(pr * LANES, (pr + 1) * LANES)
            q = q_ref[rows, cols]
            q2 = jnp.concatenate([q * low_b, q * high_b], axis=0)
            if qb == 0:
                kcat = jnp.concatenate([kprev_ref[:, cols], k_ref[0:BLOCK, cols]], axis=0)
                vcat = jnp.concatenate([vprev_ref[:, cols], v_ref[0:BLOCK, cols]], axis=0)
                bias = bias_ref[first, pr]
            else:
                krows = slice((qb - 1) * BLOCK, (qb + 1) * BLOCK)
                kcat = k_ref[krows, cols]
                vcat = v_ref[krows, cols]
                bias = bias_ref[0, pr]
            s = lax.dot_general(q2, kcat, (((1,), (1,)), ((), ())),
                                preferred_element_type=F32) + bias
            m = jnp.max(s, axis=1, keepdims=True)
            p = jnp.exp(s - m)
            l = jnp.sum(p, axis=1, keepdims=True)
            o2 = jnp.dot(p.astype(BF16), vcat, preferred_element_type=F32)
            o2 = o2 * (1.0 / l)
            o = jnp.where(low, o2[0:BLOCK], o2[BLOCK:2 * BLOCK])
            o_ref[rows, cols] = o.astype(BF16)
            lse = m + jnp.log(l)
            lse_tile = jnp.where(lane == HEADS_PER_TILE * pr, lse[0:BLOCK], lse_tile)
            lse_tile = jnp.where(lane == HEADS_PER_TILE * pr + 1,
                                 lse[BLOCK:2 * BLOCK], lse_tile)
        lse_ref[rows, :] = lse_tile

    kprev_ref[...] = k_ref[(nq - 1) * BLOCK:nq * BLOCK, :]
    vprev_ref[...] = v_ref[(nq - 1) * BLOCK:nq * BLOCK, :]


def _group_bias(rel_bias, g, dilation, span):
    a = jnp.arange(BLOCK)[:, None]
    kk = jnp.arange(2 * BLOCK)[None, :]
    step = BLOCK + a - kk
    valid = (step >= 0) & (step <= span)
    dist = jnp.clip(step, 0, span) * dilation
    max_exact = N_BUCKETS // 2
    d = jnp.maximum(dist, 1).astype(F32)
    large = max_exact + (jnp.log(d / max_exact) / math.log(MAX_DISTANCE / max_exact)
                         * (N_BUCKETS - max_exact)).astype(jnp.int32)
    large = jnp.minimum(large, N_BUCKETS - 1)
    bucket = jnp.where(dist < max_exact, dist, large)
    tab = rel_bias[:, g * N_HEADS:(g + 1) * N_HEADS].astype(F32)
    bias = tab[bucket].transpose(2, 0, 1)
    interior = jnp.where(valid[None], bias, MASKED)
    first = jnp.where((valid & (kk >= BLOCK))[None], bias, MASKED)
    both = jnp.stack([interior, first])
    return both.reshape(2, N_PAIRS, HEADS_PER_TILE * BLOCK, 2 * BLOCK)


def _attn_group(qkv, rel_bias, g, batch, seq):
    window, dilation = DILATED_GROUPS[g]
    span = window // dilation
    sub_len = seq // dilation
    nq = min(4, sub_len // BLOCK)
    rows = nq * BLOCK
    tiles = ATTN_WIDTH // ATTN_WIDTH
    row_tiles = (QKV_COLS // ATTN_WIDTH)
    qkv_v = qkv.reshape(batch, sub_len, dilation * QKV_COLS)
    bias = _group_bias(rel_bias, g, dilation, span)

    def part_spec(part):
        return pl.BlockSpec(
            (None, rows, ATTN_WIDTH),
            lambda b, c, n: (b, n, c * row_tiles + g * 3 + part))

    out, lse = pl.pallas_call(
        functools.partial(_attn_group_kernel, nq=nq),
        grid=(batch, dilation, sub_len // rows),
        in_specs=[part_spec(0), part_spec(1), part_spec(2),
                  _resident(bias.shape)],
        out_specs=[pl.BlockSpec((None, rows, ATTN_WIDTH), lambda b, c, n: (b, n, c * tiles)),
                   pl.BlockSpec((None, rows, LANES), lambda b, c, n: (b, n, c))],
        out_shape=[jax.ShapeDtypeStruct((batch, sub_len, dilation * ATTN_WIDTH), BF16),
                   jax.ShapeDtypeStruct((batch, sub_len, dilation * LANES), F32)],
        scratch_shapes=[pltpu.VMEM((BLOCK, ATTN_WIDTH), BF16),
                        pltpu.VMEM((BLOCK, ATTN_WIDTH), BF16)],
        compiler_params=pltpu.CompilerParams(
            dimension_semantics=("arbitrary", "arbitrary", "arbitrary"),
            vmem_limit_bytes=VMEM_LIMIT),
        name=f"attn_group{g}",
    )(qkv_v, qkv_v, qkv_v, bias)
    t = batch * seq
    return out.reshape(t, ATTN_WIDTH), lse.reshape(t, LANES)


def _merge_out_kernel(x_ref, o0_ref, o1_ref, o2_ref, l0_ref, l1_ref, l2_ref,
                      gate_ref, expand_ref, w_ref, out_ref):
    lses = [l0_ref[...], l1_ref[...], l2_ref[...]]
    m = jnp.maximum(jnp.maximum(lses[0], lses[1]), lses[2])
    es = [jnp.exp(l - m) for l in lses]
    inv = 1.0 / (es[0] + es[1] + es[2])
    o = None
    for e, o_ref in zip(es, (o0_ref, o1_ref, o2_ref)):
        alpha = e * inv
        hi = alpha.astype(BF16)
        lo = (alpha - hi.astype(F32)).astype(BF16)
        spread = jnp.dot(jnp.concatenate([hi, lo], axis=1), expand_ref[...],
                         preferred_element_type=F32)
        term = spread * o_ref[...].astype(F32)
        o = term if o is None else o + term
    y = o * gate_ref[...].astype(F32)
    out_ref[...] = x_ref[...] + jnp.dot(y.astype(BF16), w_ref[...],
                                        preferred_element_type=F32)


def _merge_out(x2, outs, lses, gate, w_out):
    t = x2.shape[0]
    tm = 512
    lane_head = jnp.arange(2 * LANES) % LANES
    expand = (lane_head[:, None] == (jnp.arange(ATTN_WIDTH) // HEAD_DIM)[None, :]).astype(BF16)
    row = lambda w: pl.BlockSpec((tm, w), lambda i: (i, 0))
    return pl.pallas_call(
        _merge_out_kernel,
        grid=(t // tm,),
        in_specs=[row(D_MODEL)] + [row(ATTN_WIDTH)] * 3 + [row(LANES)] * 3
                 + [row(ATTN_WIDTH), _resident(expand.shape),
                    _resident((ATTN_WIDTH, D_MODEL))],
        out_specs=row(D_MODEL),
        out_shape=jax.ShapeDtypeStruct((t, D_MODEL), F32),
        compiler_params=_params(),
        name="merge_out",
    )(x2, *outs, *lses, gate, expand, w_out.astype(BF16))


def _attn_layer(x2, norm_g, w_in, q_gain, k_gain, w_out, rel_bias, batch, seq):
    qkv, gate = _attn_proj(x2, norm_g, w_in, q_gain, k_gain)
    outs, lses = [], []
    for g in range(N_GROUPS):
        o, l = _attn_group(qkv, rel_bias, g, batch, seq)
        outs.append(o)
        lses.append(l)
    return _merge_out(x2, outs, lses, gate, w_out)


def kernel(x, conv_norm, conv_w_in, conv_w, conv_w_out, attn_norm, attn_w_in,
           attn_q_gain, attn_k_gain, attn_w_out, rel_bias):
    batch, seq, d = x.shape
    assert d == D_MODEL and seq % (BLOCK * DILATED_GROUPS[-1][1]) == 0
    depth = conv_norm.shape[0] + attn_norm.shape[0]
    x2 = x.reshape(batch * seq, d)
    for i in range(depth):
        j = i // 2
        if i % 2 == 0:
            x2 = _conv_layer(x2, conv_norm[j], conv_w_in[j], conv_w[j],
                             conv_w_out[j], seq)
        else:
            x2 = _attn_layer(x2, attn_norm[j], attn_w_in[j], attn_q_gain[j],
                             attn_k_gain[j], attn_w_out[j], rel_bias, batch, seq)
    return x2.reshape(batch, seq, d)
```

```python
import functools
import math

import jax
import jax.numpy as jnp
from jax import lax
from jax.experimental import pallas as pl
from jax.experimental.pallas import tpu as pltpu

D_MODEL = 1024
CONV_WIDTH = 2048
CONV_K = 3
HEAD_DIM = 64
N_HEADS = 16
ATTN_WIDTH = N_HEADS * HEAD_DIM
DILATED_GROUPS = ((128, 1), (512, 4), (2048, 16))
N_GROUPS = len(DILATED_GROUPS)
QKV_COLS = N_GROUPS * 3 * ATTN_WIDTH
BLOCK = 128
N_BUCKETS = 32
MAX_DISTANCE = 2048
EPS = 1e-6
MASKED = -1e30

LANES = 128
MXU_COLS = 256
HEADS_PER_TILE = LANES // HEAD_DIM
N_PAIRS = N_HEADS // HEADS_PER_TILE
LANE_TILES = ATTN_WIDTH // LANES
VMEM_LIMIT = 56 * 1024 * 1024
TOKEN_TILE = 512

F32 = jnp.float32
BF16 = jnp.bfloat16


def _resident(shape):
    return pl.BlockSpec(shape, lambda *_: (0,) * len(shape),
                        pipeline_mode=pl.Buffered(1))


def _params(n_axes=1):
    return pltpu.CompilerParams(
        dimension_semantics=("arbitrary",) * n_axes, vmem_limit_bytes=VMEM_LIMIT)


def _rms_rows(x, gain):
    ms = jnp.mean(x * x, axis=-1, keepdims=True)
    return x * lax.rsqrt(ms + EPS) * gain


def _conv_layer_kernel(x_ref, g_ref, win_ref, wc_ref, wout_ref, o_ref,
                       h_ref, carry_ref, *, tm, te, tiles_per_seq):
    i = pl.program_id(0)
    x = x_ref[...]
    h_ref[...] = _rms_rows(x, g_ref[...]).astype(BF16)

    @pl.when(i % tiles_per_seq == 0)
    def _():
        carry_ref[...] = jnp.zeros_like(carry_ref)

    rows = lax.broadcasted_iota(jnp.int32, (tm, te), 0)
    for j in range(CONV_WIDTH // te):
        cols = slice(j * te, (j + 1) * te)

        def proj(p):
            w = win_ref[:, p * CONV_WIDTH + j * te:p * CONV_WIDTH + (j + 1) * te]
            return jnp.dot(h_ref[...], w, preferred_element_type=F32)

        v = proj(1) * proj(2)
        prev1 = carry_ref[1:2, cols]
        prev2 = carry_ref[0:1, cols]
        v1 = jnp.where(rows == 0, prev1, pltpu.roll(v, 1, 0))
        v2 = jnp.where(rows == 0, prev2,
                       jnp.where(rows == 1, prev1, pltpu.roll(v, 2, 0)))
        carry_ref[0:2, cols] = v[tm - 2:tm]
        wc = wc_ref[:, cols]
        conv = wc[0:1] * v2 + wc[1:2] * v1 + wc[2:3] * v
        y = proj(0) * conv * jax.nn.silu(proj(3))
        part = jnp.dot(y.astype(BF16), wout_ref[cols, :],
                       preferred_element_type=F32)
        if j == 0:
            o_ref[...] = x + part
        else:
            o_ref[...] += part


def _conv_layer(x2, norm_g, w_in, w_conv, w_out, seq):
    t = x2.shape[0]
    tm, te = TOKEN_TILE, 512
    kern = functools.partial(_conv_layer_kernel, tm=tm, te=te,
                             tiles_per_seq=seq // tm)
    return pl.pallas_call(
        kern,
        grid=(t // tm,),
        in_specs=[
            pl.BlockSpec((tm, D_MODEL), lambda i: (i, 0)),
            _resident((1, D_MODEL)),
            _resident((D_MODEL, 4 * CONV_WIDTH)),
            _resident((CONV_K, CONV_WIDTH)),
            _resident((CONV_WIDTH, D_MODEL)),
        ],
        out_specs=pl.BlockSpec((tm, D_MODEL), lambda i: (i, 0)),
        out_shape=jax.ShapeDtypeStruct((t, D_MODEL), F32),
        scratch_shapes=[pltpu.VMEM((tm, D_MODEL), BF16),
                        pltpu.VMEM((8, CONV_WIDTH), F32)],
        compiler_params=_params(),
        name="conv_layer",
    )(x2, norm_g.reshape(1, D_MODEL), w_in.astype(BF16), w_conv,
      w_out.astype(BF16))


def _attn_proj_kernel(x_ref, g_ref, w_ref, qg_ref, kg_ref, bd_ref,
                      qkv0_ref, qkv1_ref, qkv2_ref, gate_ref, h_ref, slab_ref,
                      *, tm):
    h32 = _rms_rows(x_ref[...], g_ref[...])
    h_ref[0] = h32.astype(BF16)
    for j in range(LANE_TILES):
        slab_ref[j] = h32[:, j * LANES:(j + 1) * LANES]
    for g in range(1, N_GROUPS):
        r = DILATED_GROUPS[g][1]
        sub = tm // r
        for c in range(r):
            for j in range(LANE_TILES):
                piece = slab_ref[j, pl.ds(c, sub, stride=r), :]
                h_ref[g, c * sub:(c + 1) * sub, j * LANES:(j + 1) * LANES] = piece.astype(BF16)

    out_refs = (qkv0_ref, qkv1_ref, qkv2_ref)
    for g in range(N_GROUPS):
        r = DILATED_GROUPS[g][1]
        sub = tm // r
        for part in range(3):
            col0 = (g * 3 + part) * ATTN_WIDTH
            a = jnp.dot(h_ref[g], w_ref[:, col0:col0 + ATTN_WIDTH],
                        preferred_element_type=F32)
            gain_ref = (qg_ref, kg_ref, None)[part]
            for s in range(ATTN_WIDTH // MXU_COLS):
                sl = slice(s * MXU_COLS, (s + 1) * MXU_COLS)
                a_s = a[:, sl]
                if gain_ref is not None:
                    ss = jnp.dot((a_s * a_s).astype(BF16), bd_ref[...],
                                 preferred_element_type=F32)
                    a_s = a_s * lax.rsqrt(ss * (1.0 / HEAD_DIM) + EPS) * gain_ref[g:g + 1, sl]
                a_s = a_s.astype(BF16)
                for c in range(r):
                    out_refs[g][c, :, part * ATTN_WIDTH + s * MXU_COLS:
                                part * ATTN_WIDTH + (s + 1) * MXU_COLS] = a_s[c * sub:(c + 1) * sub]

    z = jnp.dot(h_ref[0], w_ref[:, QKV_COLS:QKV_COLS + ATTN_WIDTH],
                preferred_element_type=F32)
    gate_ref[...] = jax.nn.silu(z).astype(BF16)


def _dilated_block(r, sub, width, tiles_per_seq):
    return pl.BlockSpec((None, r, sub, width),
                        lambda i: (i // tiles_per_seq, 0, i % tiles_per_seq, 0))


def _attn_proj(x2, norm_g, w_in, q_gain, k_gain, batch, seq):
    t = x2.shape[0]
    tm = TOKEN_TILE
    tiles_per_seq = seq // tm
    head_of_lane = jnp.arange(MXU_COLS) // HEAD_DIM
    block_diag = (head_of_lane[:, None] == head_of_lane[None, :]).astype(BF16)
    qg = jnp.tile(q_gain, (1, N_HEADS)) * (HEAD_DIM ** -0.5)
    kg = jnp.tile(k_gain, (1, N_HEADS))
    qkv_specs, qkv_shapes = [], []
    for _, r in DILATED_GROUPS:
        qkv_specs.append(_dilated_block(r, tm // r, 3 * ATTN_WIDTH, tiles_per_seq))
        qkv_shapes.append(jax.ShapeDtypeStruct((batch, r, seq // r, 3 * ATTN_WIDTH), BF16))
    return pl.pallas_call(
        functools.partial(_attn_proj_kernel, tm=tm),
        grid=(t // tm,),
        in_specs=[
            pl.BlockSpec((tm, D_MODEL), lambda i: (i, 0)),
            _resident((1, D_MODEL)),
            _resident((D_MODEL, QKV_COLS + ATTN_WIDTH)),
            _resident((N_GROUPS, ATTN_WIDTH)),
            _resident((N_GROUPS, ATTN_WIDTH)),
            _resident((MXU_COLS, MXU_COLS)),
        ],
        out_specs=qkv_specs + [pl.BlockSpec((tm, ATTN_WIDTH), lambda i: (i, 0))],
        out_shape=qkv_shapes + [jax.ShapeDtypeStruct((t, ATTN_WIDTH), BF16)],
        scratch_shapes=[pltpu.VMEM((N_GROUPS, tm, D_MODEL), BF16),
                        pltpu.VMEM((LANE_TILES, tm, LANES), F32)],
        compiler_params=_params(),
        name="attn_proj",
    )(x2, norm_g.reshape(1, D_MODEL), w_in.astype(BF16), qg, kg, block_diag)


def _attn_group_kernel(q_ref, k_ref, v_ref, bias_ref, o_ref, lse_ref,
                       kprev_ref, vprev_ref, *, nq):
    n = pl.program_id(2)

    @pl.when(n == 0)
    def _():
        kprev_ref[...] = jnp.zeros_like(kprev_ref)
        vprev_ref[...] = jnp.zeros_like(vprev_ref)

    first = jnp.where(n == 0, 1, 0)
    lane = lax.broadcasted_iota(jnp.int32, (BLOCK, LANES), 1)
    low = lane < HEAD_DIM
    lane_row = lax.broadcasted_iota(jnp.int32, (1, LANES), 1)
    low_b = (lane_row < HEAD_DIM).astype(F32).astype(BF16)
    high_b = (lane_row >= HEAD_DIM).astype(F32).astype(BF16)

    for qb in range(nq):
        rows = slice(qb * BLOCK, (qb + 1) * BLOCK)
        lse_tile = jnp.zeros((BLOCK, LANES), F32)
        for pr in range(N_PAIRS):
            cols = slice(pr * LANES, (pr + 1) * LANES)
            q = q_ref[rows, cols]
            q2 = jnp.concatenate([q * low_b, q * high_b], axis=0)
            if qb == 0:
                kcat = jnp.concatenate([kprev_ref[:, cols], k_ref[0:BLOCK, cols]], axis=0)
                vcat = jnp.concatenate([vprev_ref[:, cols], v_ref[0:BLOCK, cols]], axis=0)
                bias = bias_ref[first, pr]
            else:
                krows = slice((qb - 1) * BLOCK, (qb + 1) * BLOCK)
                kcat = k_ref[krows, cols]
                vcat = v_ref[krows, cols]
                bias = bias_ref[0, pr]
            s = lax.dot_general(q2, kcat, (((1,), (1,)), ((), ())),
                                preferred_element_type=F32) + bias
            m = jnp.max(s, axis=1, keepdims=True)
            p = jnp.exp(s - m)
            l = jnp.sum(p, axis=1, keepdims=True)
            o2 = jnp.dot(p.astype(BF16), vcat, preferred_element_type=F32)
            o2 = o2 * (1.0 / l)
            o = jnp.where(low, o2[0:BLOCK], o2[BLOCK:2 * BLOCK])
            o_ref[rows, cols] = o.astype(BF16)
            lse = m + jnp.log(l)
            lse_tile = jnp.where(lane == HEADS_PER_TILE * pr, lse[0:BLOCK], lse_tile)
            lse_tile = jnp.where(lane == HEADS_PER_TILE * pr + 1,
                                 lse[BLOCK:2 * BLOCK], lse_tile)
        lse_ref[rows, :] = lse_tile

    kprev_ref[...] = k_ref[(nq - 1) * BLOCK:nq * BLOCK, :]
    vprev_ref[...] = v_ref[(nq - 1) * BLOCK:nq * BLOCK, :]


def _group_bias(rel_bias, g, dilation, span):
    a = jnp.arange(BLOCK)[:, None]
    kk = jnp.arange(2 * BLOCK)[None, :]
    step = BLOCK + a - kk
    valid = (step >= 0) & (step <= span)
    dist = jnp.clip(step, 0, span) * dilation
    max_exact = N_BUCKETS // 2
    d = jnp.maximum(dist, 1).astype(F32)
    large = max_exact + (jnp.log(d / max_exact) / math.log(MAX_DISTANCE / max_exact)
                         * (N_BUCKETS - max_exact)).astype(jnp.int32)
    large = jnp.minimum(large, N_BUCKETS - 1)
    bucket = jnp.where(dist < max_exact, dist, large)
    tab = rel_bias[:, g * N_HEADS:(g + 1) * N_HEADS].astype(F32)
    onehot = (bucket[..., None] == jnp.arange(N_BUCKETS)).astype(F32)
    bias = jnp.einsum("akn,nh->hak", onehot, tab, precision=lax.Precision.HIGHEST)
    interior = jnp.where(valid[None], bias, MASKED)
    first = jnp.where((valid & (kk >= BLOCK))[None], bias, MASKED)
    both = jnp.stack([interior, first])
    return both.reshape(2, N_PAIRS, HEADS_PER_TILE * BLOCK, 2 * BLOCK)


def _attn_group(qkv, rel_bias, g):
    batch, dilation, sub_len, _ = qkv.shape
    span = DILATED_GROUPS[g][0] // dilation
    nq = min(4, sub_len // BLOCK)
    rows = nq * BLOCK
    bias = _group_bias(rel_bias, g, dilation, span)

    def part_spec(part):
        return pl.BlockSpec((None, None, rows, ATTN_WIDTH),
                            lambda b, c, n: (b, c, n, part))

    return pl.pallas_call(
        functools.partial(_attn_group_kernel, nq=nq),
        grid=(batch, dilation, sub_len // rows),
        in_specs=[part_spec(0), part_spec(1), part_spec(2),
                  _resident(bias.shape)],
        out_specs=[pl.BlockSpec((None, None, rows, ATTN_WIDTH), lambda b, c, n: (b, c, n, 0)),
                   pl.BlockSpec((None, None, rows, LANES), lambda b, c, n: (b, c, n, 0))],
        out_shape=[jax.ShapeDtypeStruct((batch, dilation, sub_len, ATTN_WIDTH), BF16),
                   jax.ShapeDtypeStruct((batch, dilation, sub_len, LANES), F32)],
        scratch_shapes=[pltpu.VMEM((BLOCK, ATTN_WIDTH), BF16),
                        pltpu.VMEM((BLOCK, ATTN_WIDTH), BF16)],
        compiler_params=_params(3),
        name=f"attn_group{g}",
    )(qkv, qkv, qkv, bias)


def _merge_out_kernel(x_ref, o0_ref, o1_ref, o2_ref, l0_ref, l1_ref, l2_ref,
                      gate_ref, expand_ref, w_ref, out_ref,
                      lnat_ref, onat_ref, y_ref, *, tm):
    o_refs = (o0_ref, o1_ref, o2_ref)
    l_refs = (l0_ref, l1_ref, l2_ref)
    for g in range(N_GROUPS):
        r = DILATED_GROUPS[g][1]
        sub = tm // r
        for c in range(r):
            lnat_ref[g, pl.ds(c, sub, stride=r), :] = l_refs[g][c]
            for j in range(LANE_TILES):
                onat_ref[g, j, pl.ds(c, sub, stride=r), :] = (
                    o_refs[g][c, :, j * LANES:(j + 1) * LANES].astype(F32))

    lses = [lnat_ref[g] for g in range(N_GROUPS)]
    m = jnp.maximum(jnp.maximum(lses[0], lses[1]), lses[2])
    es = [jnp.exp(l - m) for l in lses]
    inv = 1.0 / (es[0] + es[1] + es[2])
    spreads = []
    for e in es:
        alpha = e * inv
        hi = alpha.astype(BF16)
        lo = (alpha - hi.astype(F32)).astype(BF16)
        spreads.append(jnp.dot(jnp.concatenate([hi, lo], axis=1), expand_ref[...],
                               preferred_element_type=F32))
    for j in range(LANE_TILES):
        cols = slice(j * LANES, (j + 1) * LANES)
        o = sum(spreads[g][:, cols] * onat_ref[g, j] for g in range(N_GROUPS))
        y_ref[:, cols] = (o * gate_ref[:, cols].astype(F32)).astype(BF16)
    out_ref[...] = x_ref[...] + jnp.dot(y_ref[...], w_ref[...],
                                        preferred_element_type=F32)


def _merge_out(x2, outs, lses, gate, w_out, seq):
    t = x2.shape[0]
    tm = TOKEN_TILE
    tiles_per_seq = seq // tm
    lane_head = jnp.arange(2 * LANES) % LANES
    expand = (lane_head[:, None] == (jnp.arange(ATTN_WIDTH) // HEAD_DIM)[None, :]).astype(BF16)
    row = lambda w: pl.BlockSpec((tm, w), lambda i: (i, 0))
    o_specs = [_dilated_block(r, tm // r, ATTN_WIDTH, tiles_per_seq) for _, r in DILATED_GROUPS]
    l_specs = [_dilated_block(r, tm // r, LANES, tiles_per_seq) for _, r in DILATED_GROUPS]
    return pl.pallas_call(
        functools.partial(_merge_out_kernel, tm=tm),
        grid=(t // tm,),
        in_specs=[row(D_MODEL)] + o_specs + l_specs
                 + [row(ATTN_WIDTH), _resident(expand.shape),
                    _resident((ATTN_WIDTH, D_MODEL))],
        out_specs=row(D_MODEL),
        out_shape=jax.ShapeDtypeStruct((t, D_MODEL), F32),
        scratch_shapes=[pltpu.VMEM((N_GROUPS, tm, LANES), F32),
                        pltpu.VMEM((N_GROUPS, LANE_TILES, tm, LANES), F32),
                        pltpu.VMEM((tm, ATTN_WIDTH), BF16)],
        compiler_params=_params(),
        name="merge_out",
    )(x2, *outs, *lses, gate, expand, w_out.astype(BF16))


def _attn_layer(x2, norm_g, w_in, q_gain, k_gain, w_out, rel_bias, batch, seq):
    *qkvs, gate = _attn_proj(x2, norm_g, w_in, q_gain, k_gain, batch, seq)
    outs, lses = [], []
    for g in range(N_GROUPS):
        o, l = _attn_group(qkvs[g], rel_bias, g)
        outs.append(o)
        lses.append(l)
    return _merge_out(x2, outs, lses, gate, w_out, seq)


def kernel(x, conv_norm, conv_w_in, conv_w, conv_w_out, attn_norm, attn_w_in,
           attn_q_gain, attn_k_gain, attn_w_out, rel_bias):
    batch, seq, d = x.shape
    assert d == D_MODEL and seq % TOKEN_TILE == 0
    assert all(TOKEN_TILE % (8 * r) == 0 and seq % (BLOCK * r) == 0
               for _, r in DILATED_GROUPS)
    depth = conv_norm.shape[0] + attn_norm.shape[0]
    x2 = x.reshape(batch * seq, d)
    for i in range(depth):
        j = i // 2
        if i % 2 == 0:
            x2 = _conv_layer(x2, conv_norm[j], conv_w_in[j], conv_w[j],
                             conv_w_out[j], seq)
        else:
            x2 = _attn_layer(x2, attn_norm[j], attn_w_in[j], attn_q_gain[j],
                             attn_k_gain[j], attn_w_out[j], rel_bias, batch, seq)
    return x2.reshape(batch, seq, d)
```

```python
import functools
import math

import jax
import jax.numpy as jnp
from jax import lax
from jax.experimental import pallas as pl
from jax.experimental.pallas import tpu as pltpu

D_MODEL = 1024
CONV_WIDTH = 2048
CONV_K = 3
HEAD_DIM = 64
N_HEADS = 16
ATTN_WIDTH = N_HEADS * HEAD_DIM
DILATED_GROUPS = ((128, 1), (512, 4), (2048, 16))
N_GROUPS = len(DILATED_GROUPS)
QKV_COLS = N_GROUPS * 3 * ATTN_WIDTH
BLOCK = 128
N_BUCKETS = 32
MAX_DISTANCE = 2048
EPS = 1e-6
MASKED = -1e30

LANES = 128
MXU_COLS = 256
HEADS_PER_TILE = LANES // HEAD_DIM
N_PAIRS = N_HEADS // HEADS_PER_TILE
LANE_TILES = ATTN_WIDTH // LANES
VMEM_LIMIT = 56 * 1024 * 1024
TOKEN_TILE = 512
LOG2E = math.log2(math.e)
LN2 = math.log(2.0)

F32 = jnp.float32
BF16 = jnp.bfloat16


def _resident(shape):
    return pl.BlockSpec(shape, lambda *_: (0,) * len(shape),
                        pipeline_mode=pl.Buffered(1))


def _params(n_axes=1):
    return pltpu.CompilerParams(
        dimension_semantics=("arbitrary",) * n_axes, vmem_limit_bytes=VMEM_LIMIT)


def _rms_rows(x, gain):
    ms = jnp.mean(x * x, axis=-1, keepdims=True)
    return x * lax.rsqrt(ms + EPS) * gain


def _conv_layer_kernel(x_ref, g_ref, win_ref, wc_ref, wout_ref, o_ref,
                       h_ref, carry_ref, *, tm, te, tiles_per_seq):
    i = pl.program_id(0)
    x = x_ref[...]
    h_ref[...] = _rms_rows(x, g_ref[...]).astype(BF16)

    @pl.when(i % tiles_per_seq == 0)
    def _():
        carry_ref[...] = jnp.zeros_like(carry_ref)

    rows = lax.broadcasted_iota(jnp.int32, (tm, te), 0)
    for j in range(CONV_WIDTH // te):
        cols = slice(j * te, (j + 1) * te)

        def proj(p):
            w = win_ref[:, p * CONV_WIDTH + j * te:p * CONV_WIDTH + (j + 1) * te]
            return jnp.dot(h_ref[...], w, preferred_element_type=F32)

        v = proj(1) * proj(2)
        prev1 = carry_ref[1:2, cols]
        prev2 = carry_ref[0:1, cols]
        v1 = jnp.where(rows == 0, prev1, pltpu.roll(v, 1, 0))
        v2 = jnp.where(rows == 0, prev2,
                       jnp.where(rows == 1, prev1, pltpu.roll(v, 2, 0)))
        carry_ref[0:2, cols] = v[tm - 2:tm]
        wc = wc_ref[:, cols]
        conv = wc[0:1] * v2 + wc[1:2] * v1 + wc[2:3] * v
        y = proj(0) * conv * jax.nn.silu(proj(3))
        part = jnp.dot(y.astype(BF16), wout_ref[cols, :],
                       preferred_element_type=F32)
        if j == 0:
            o_ref[...] = x + part
        else:
            o_ref[...] += part


def _conv_layer(x2, norm_g, w_in, w_conv, w_out, seq):
    t = x2.shape[0]
    tm, te = TOKEN_TILE, 512
    kern = functools.partial(_conv_layer_kernel, tm=tm, te=te,
                             tiles_per_seq=seq // tm)
    return pl.pallas_call(
        kern,
        grid=(t // tm,),
        in_specs=[
            pl.BlockSpec((tm, D_MODEL), lambda i: (i, 0)),
            _resident((1, D_MODEL)),
            _resident((D_MODEL, 4 * CONV_WIDTH)),
            _resident((CONV_K, CONV_WIDTH)),
            _resident((CONV_WIDTH, D_MODEL)),
        ],
        out_specs=pl.BlockSpec((tm, D_MODEL), lambda i: (i, 0)),
        out_shape=jax.ShapeDtypeStruct((t, D_MODEL), F32),
        scratch_shapes=[pltpu.VMEM((tm, D_MODEL), BF16),
                        pltpu.VMEM((8, CONV_WIDTH), F32)],
        compiler_params=_params(),
        name="conv_layer",
    )(x2, norm_g.reshape(1, D_MODEL), w_in.astype(BF16), w_conv,
      w_out.astype(BF16))


def _attn_proj_kernel(x_ref, g_ref, w_ref, qg_ref, kg_ref, bd_ref,
                      qkv0_ref, qkv1_ref, qkv2_ref, gate_ref, h_ref, slab_ref,
                      *, tm):
    h32 = _rms_rows(x_ref[...], g_ref[...])
    h_ref[0] = h32.astype(BF16)
    for j in range(LANE_TILES):
        slab_ref[j] = h32[:, j * LANES:(j + 1) * LANES]
    for g in range(1, N_GROUPS):
        r = DILATED_GROUPS[g][1]
        sub = tm // r
        for c in range(r):
            for j in range(LANE_TILES):
                piece = slab_ref[j, pl.ds(c, sub, stride=r), :]
                h_ref[g, c * sub:(c + 1) * sub, j * LANES:(j + 1) * LANES] = piece.astype(BF16)

    out_refs = (qkv0_ref, qkv1_ref, qkv2_ref)
    for g in range(N_GROUPS):
        r = DILATED_GROUPS[g][1]
        sub = tm // r
        for part in range(3):
            col0 = (g * 3 + part) * ATTN_WIDTH
            a = jnp.dot(h_ref[g], w_ref[:, col0:col0 + ATTN_WIDTH],
                        preferred_element_type=F32)
            gain_ref = (qg_ref, kg_ref, None)[part]
            for s in range(ATTN_WIDTH // MXU_COLS):
                sl = slice(s * MXU_COLS, (s + 1) * MXU_COLS)
                a_s = a[:, sl]
                if gain_ref is not None:
                    ss = jnp.dot((a_s * a_s).astype(BF16), bd_ref[...],
                                 preferred_element_type=F32)
                    a_s = a_s * lax.rsqrt(ss * (1.0 / HEAD_DIM) + EPS) * gain_ref[g:g + 1, sl]
                a_s = a_s.astype(BF16)
                for c in range(r):
                    out_refs[g][c, :, part * ATTN_WIDTH + s * MXU_COLS:
                                part * ATTN_WIDTH + (s + 1) * MXU_COLS] = a_s[c * sub:(c + 1) * sub]

    z = jnp.dot(h_ref[0], w_ref[:, QKV_COLS:QKV_COLS + ATTN_WIDTH],
                preferred_element_type=F32)
    gate_ref[...] = jax.nn.silu(z).astype(BF16)


def _dilated_block(r, sub, width, tiles_per_seq):
    return pl.BlockSpec((None, r, sub, width),
                        lambda i: (i // tiles_per_seq, 0, i % tiles_per_seq, 0))


def _attn_proj(x2, norm_g, w_in, q_gain, k_gain, batch, seq):
    t = x2.shape[0]
    tm = TOKEN_TILE
    tiles_per_seq = seq // tm
    head_of_lane = jnp.arange(MXU_COLS) // HEAD_DIM
    block_diag = (head_of_lane[:, None] == head_of_lane[None, :]).astype(BF16)
    qg = jnp.tile(q_gain, (1, N_HEADS)) * (HEAD_DIM ** -0.5 * LOG2E)
    kg = jnp.tile(k_gain, (1, N_HEADS))
    qkv_specs, qkv_shapes = [], []
    for _, r in DILATED_GROUPS:
        qkv_specs.append(_dilated_block(r, tm // r, 3 * ATTN_WIDTH, tiles_per_seq))
        qkv_shapes.append(jax.ShapeDtypeStruct((batch, r, seq // r, 3 * ATTN_WIDTH), BF16))
    return pl.pallas_call(
        functools.partial(_attn_proj_kernel, tm=tm),
        grid=(t // tm,),
        in_specs=[
            pl.BlockSpec((tm, D_MODEL), lambda i: (i, 0)),
            _resident((1, D_MODEL)),
            _resident((D_MODEL, QKV_COLS + ATTN_WIDTH)),
            _resident((N_GROUPS, ATTN_WIDTH)),
            _resident((N_GROUPS, ATTN_WIDTH)),
            _resident((MXU_COLS, MXU_COLS)),
        ],
        out_specs=qkv_specs + [pl.BlockSpec((tm, ATTN_WIDTH), lambda i: (i, 0))],
        out_shape=qkv_shapes + [jax.ShapeDtypeStruct((t, ATTN_WIDTH), BF16)],
        scratch_shapes=[pltpu.VMEM((N_GROUPS, tm, D_MODEL), BF16),
                        pltpu.VMEM((LANE_TILES, tm, LANES), F32)],
        compiler_params=_params(),
        name="attn_proj",
    )(x2, norm_g.reshape(1, D_MODEL), w_in.astype(BF16), qg, kg, block_diag)


def _attn_group_kernel(q_ref, k_ref, v_ref, bias_ref, o_ref, stat_ref,
                       kprev_ref, vprev_ref, *, nq):
    n = pl.program_id(2)

    @pl.when(n == 0)
    def _():
        kprev_ref[...] = jnp.zeros_like(kprev_ref)
        vprev_ref[...] = jnp.zeros_like(vprev_ref)

    first = jnp.where(n == 0, 1, 0)
    lane = lax.broadcasted_iota(jnp.int32, (BLOCK, LANES), 1)
    low = lane < HEAD_DIM
    lane_row = lax.broadcasted_iota(jnp.int32, (1, LANES), 1)
    low_b = (lane_row < HEAD_DIM).astype(F32).astype(BF16)
    high_b = (lane_row >= HEAD_DIM).astype(F32).astype(BF16)
    ones = jnp.ones((2 * BLOCK, LANES), BF16)

    for qb in range(nq):
        rows = slice(qb * BLOCK, (qb + 1) * BLOCK)
        stat_tile = jnp.zeros((BLOCK, LANES), F32)
        for pr in range(N_PAIRS):
            cols = slice(pr * LANES, (pr + 1) * LANES)
            q = q_ref[rows, cols]
            q2 = jnp.concatenate([q * low_b, q * high_b], axis=0)
            if qb == 0:
                kcat = jnp.concatenate([kprev_ref[:, cols], k_ref[0:BLOCK, cols]], axis=0)
                vcat = jnp.concatenate([vprev_ref[:, cols], v_ref[0:BLOCK, cols]], axis=0)
                bidx = first
            else:
                krows = slice((qb - 1) * BLOCK, (qb + 1) * BLOCK)
                kcat = k_ref[krows, cols]
                vcat = v_ref[krows, cols]
                bidx = 0
            s = lax.dot_general(q2, kcat, (((1,), (1,)), ((), ())),
                                preferred_element_type=F32) + bias_ref[bidx, pr]
            m = jnp.max(s, axis=1, keepdims=True)
            p = jnp.exp2(s - m).astype(BF16)
            oe = jnp.dot(p, jnp.concatenate([vcat, ones], axis=1),
                         preferred_element_type=F32)
            o = jnp.where(low, oe[0:BLOCK, :LANES], oe[BLOCK:2 * BLOCK, :LANES])
            o_ref[rows, cols] = o.astype(BF16)
            l = oe[:, LANES:]
            for hh in range(HEADS_PER_TILE):
                head = HEADS_PER_TILE * pr + hh
                hrows = slice(hh * BLOCK, (hh + 1) * BLOCK)
                stat_tile = jnp.where(lane == head, m[hrows], stat_tile)
                stat_tile = jnp.where(lane == N_HEADS + head, l[hrows], stat_tile)
        stat_ref[rows, :] = stat_tile

    kprev_ref[...] = k_ref[(nq - 1) * BLOCK:nq * BLOCK, :]
    vprev_ref[...] = v_ref[(nq - 1) * BLOCK:nq * BLOCK, :]


def _group_bias(rel_bias, g, dilation, span):
    a = jnp.arange(BLOCK)[:, None]
    kk = jnp.arange(2 * BLOCK)[None, :]
    step = BLOCK + a - kk
    valid = (step >= 0) & (step <= span)
    dist = jnp.clip(step, 0, span) * dilation
    max_exact = N_BUCKETS // 2
    d = jnp.maximum(dist, 1).astype(F32)
    large = max_exact + (jnp.log(d / max_exact) / math.log(MAX_DISTANCE / max_exact)
                         * (N_BUCKETS - max_exact)).astype(jnp.int32)
    large = jnp.minimum(large, N_BUCKETS - 1)
    bucket = jnp.where(dist < max_exact, dist, large)
    tab = rel_bias[:, g * N_HEADS:(g + 1) * N_HEADS].astype(F32)
    onehot = (bucket[..., None] == jnp.arange(N_BUCKETS)).astype(F32)
    bias = jnp.einsum("akn,nh->hak", onehot, tab, precision=lax.Precision.HIGHEST) * LOG2E
    interior = jnp.where(valid[None], bias, MASKED)
    first = jnp.where((valid & (kk >= BLOCK))[None], bias, MASKED)
    both = jnp.stack([interior, first])
    return both.reshape(2, N_PAIRS, HEADS_PER_TILE * BLOCK, 2 * BLOCK)


def _attn_group(qkv, rel_bias, g):
    batch, dilation, sub_len, _ = qkv.shape
    span = DILATED_GROUPS[g][0] // dilation
    nq = min(4, sub_len // BLOCK)
    rows = nq * BLOCK
    bias = _group_bias(rel_bias, g, dilation, span)

    def part_spec(part):
        return pl.BlockSpec((None, None, rows, ATTN_WIDTH),
                            lambda b, c, n: (b, c, n, part))

    return pl.pallas_call(
        functools.partial(_attn_group_kernel, nq=nq),
        grid=(batch, dilation, sub_len // rows),
        in_specs=[part_spec(0), part_spec(1), part_spec(2),
                  _resident(bias.shape)],
        out_specs=[pl.BlockSpec((None, None, rows, ATTN_WIDTH), lambda b, c, n: (b, c, n, 0)),
                   pl.BlockSpec((None, None, rows, LANES), lambda b, c, n: (b, c, n, 0))],
        out_shape=[jax.ShapeDtypeStruct((batch, dilation, sub_len, ATTN_WIDTH), BF16),
                   jax.ShapeDtypeStruct((batch, dilation, sub_len, LANES), F32)],
        scratch_shapes=[pltpu.VMEM((BLOCK, ATTN_WIDTH), BF16),
                        pltpu.VMEM((BLOCK, ATTN_WIDTH), BF16)],
        compiler_params=_params(3),
        name=f"attn_group{g}",
    )(qkv, qkv, qkv, bias)


def _merge_out_kernel(x_ref, o0_ref, o1_ref, o2_ref, s0_ref, s1_ref, s2_ref,
                      gate_ref, expand_ref, w_ref, out_ref,
                      snat_ref, onat_ref, y_ref, *, tm):
    o_refs = (o0_ref, o1_ref, o2_ref)
    s_refs = (s0_ref, s1_ref, s2_ref)
    for g in range(N_GROUPS):
        r = DILATED_GROUPS[g][1]
        sub = tm // r
        for c in range(r):
            snat_ref[g, pl.ds(c, sub, stride=r), :] = s_refs[g][c]
            for j in range(LANE_TILES):
                onat_ref[g, j, pl.ds(c, sub, stride=r), :] = (
                    o_refs[g][c, :, j * LANES:(j + 1) * LANES].astype(F32))

    stats = [snat_ref[g] for g in range(N_GROUPS)]
    m = jnp.maximum(jnp.maximum(stats[0], stats[1]), stats[2])
    es = [jnp.exp2(st - m) for st in stats]
    den = sum(e * pltpu.roll(st, LANES - N_HEADS, 1) for e, st in zip(es, stats))
    head_lane = lax.broadcasted_iota(jnp.int32, (tm, LANES), 1) < N_HEADS
    inv = 1.0 / jnp.where(head_lane, den, 1.0)
    spreads = []
    for e in es:
        alpha = jnp.where(head_lane, e * inv, 0.0)
        hi = alpha.astype(BF16)
        lo = (alpha - hi.astype(F32)).astype(BF16)
        spreads.append(jnp.dot(jnp.concatenate([hi, lo], axis=1), expand_ref[...],
                               preferred_element_type=F32))
    for j in range(LANE_TILES):
        cols = slice(j * LANES, (j + 1) * LANES)
        o = sum(spreads[g][:, cols] * onat_ref[g, j] for g in range(N_GROUPS))
        y_ref[:, cols] = (o * gate_ref[:, cols].astype(F32)).astype(BF16)
    out_ref[...] = x_ref[...] + jnp.dot(y_ref[...], w_ref[...],
                                        preferred_element_type=F32)


def _merge_out(x2, outs, lses, gate, w_out, seq):
    t = x2.shape[0]
    tm = TOKEN_TILE
    tiles_per_seq = seq // tm
    lane_head = jnp.arange(2 * LANES) % LANES
    expand = (lane_head[:, None] == (jnp.arange(ATTN_WIDTH) // HEAD_DIM)[None, :]).astype(BF16)
    row = lambda w: pl.BlockSpec((tm, w), lambda i: (i, 0))
    o_specs = [_dilated_block(r, tm // r, ATTN_WIDTH, tiles_per_seq) for _, r in DILATED_GROUPS]
    l_specs = [_dilated_block(r, tm // r, LANES, tiles_per_seq) for _, r in DILATED_GROUPS]
    return pl.pallas_call(
        functools.partial(_merge_out_kernel, tm=tm),
        grid=(t // tm,),
        in_specs=[row(D_MODEL)] + o_specs + l_specs
                 + [row(ATTN_WIDTH), _resident(expand.shape),
                    _resident((ATTN_WIDTH, D_MODEL))],
        out_specs=row(D_MODEL),
        out_shape=jax.ShapeDtypeStruct((t, D_MODEL), F32),
        scratch_shapes=[pltpu.VMEM((N_GROUPS, tm, LANES), F32),
                        pltpu.VMEM((N_GROUPS, LANE_TILES, tm, LANES), F32),
                        pltpu.VMEM((tm, ATTN_WIDTH), BF16)],
        compiler_params=_params(),
        name="merge_out",
    )(x2, *outs, *lses, gate, expand, w_out.astype(BF16))


def _attn_layer(x2, norm_g, w_in, q_gain, k_gain, w_out, rel_bias, batch, seq):
    *qkvs, gate = _attn_proj(x2, norm_g, w_in, q_gain, k_gain, batch, seq)
    outs, lses = [], []
    for g in range(N_GROUPS):
        o, l = _attn_group(qkvs[g], rel_bias, g)
        outs.append(o)
        lses.append(l)
    return _merge_out(x2, outs, lses, gate, w_out, seq)


def kernel(x, conv_norm, conv_w_in, conv_w, conv_w_out, attn_norm, attn_w_in,
           attn_q_gain, attn_k_gain, attn_w_out, rel_bias):
    batch, seq, d = x.shape
    assert d == D_MODEL and seq % TOKEN_TILE == 0
    assert all(TOKEN_TILE % (8 * r) == 0 and seq % (BLOCK * r) == 0
               for _, r in DILATED_GROUPS)
    depth = conv_norm.shape[0] + attn_norm.shape[0]
    x2 = x.reshape(batch * seq, d)
    for i in range(depth):
        j = i // 2
        if i % 2 == 0:
            x2 = _conv_layer(x2, conv_norm[j], conv_w_in[j], conv_w[j],
                             conv_w_out[j], seq)
        else:
            x2 = _attn_layer(x2, attn_norm[j], attn_w_in[j], attn_q_gain[j],
                             attn_k_gain[j], attn_w_out[j], rel_bias, batch, seq)
    return x2.reshape(batch, seq, d)
```

```python
import functools
import math

import jax
import jax.numpy as jnp
from jax import lax
from jax.experimental import pallas as pl
from jax.experimental.pallas import tpu as pltpu

D_MODEL = 1024
CONV_WIDTH = 2048
CONV_K = 3
HEAD_DIM = 64
N_HEADS = 16
ATTN_WIDTH = N_HEADS * HEAD_DIM
DILATED_GROUPS = ((128, 1), (512, 4), (2048, 16))
N_GROUPS = len(DILATED_GROUPS)
QKV_COLS = N_GROUPS * 3 * ATTN_WIDTH
BLOCK = 128
N_BUCKETS = 32
MAX_DISTANCE = 2048
EPS = 1e-6
MASKED = -1e30

LANES = 128
MXU_COLS = 256
HEADS_PER_TILE = LANES // HEAD_DIM
N_PAIRS = N_HEADS // HEADS_PER_TILE
LANE_TILES = ATTN_WIDTH // LANES
VMEM_LIMIT = 56 * 1024 * 1024
TOKEN_TILE = 512
LOG2E = math.log2(math.e)

F32 = jnp.float32
BF16 = jnp.bfloat16


def _resident(shape, layer=None):
    if layer is None:
        return pl.BlockSpec(shape, lambda *_: (0,) * len(shape),
                            pipeline_mode=pl.Buffered(1))
    return pl.BlockSpec((None,) + tuple(shape), lambda *_: (layer,) + (0,) * len(shape),
                        pipeline_mode=pl.Buffered(1))


def _params(n_axes=1):
    return pltpu.CompilerParams(
        dimension_semantics=("arbitrary",) * n_axes, vmem_limit_bytes=VMEM_LIMIT)


def _rms_rows(x, gain):
    ms = jnp.mean(x * x, axis=-1, keepdims=True)
    return x * lax.rsqrt(ms + EPS) * gain


def _conv_layer_kernel(x_ref, g_ref, win_ref, wc_ref, wout_ref, o_ref,
                       h_ref, carry_ref, *, tm, te, tiles_per_seq):
    i = pl.program_id(0)
    x = x_ref[...]
    h_ref[...] = _rms_rows(x, g_ref[...]).astype(BF16)

    @pl.when(i % tiles_per_seq == 0)
    def _():
        carry_ref[...] = jnp.zeros_like(carry_ref)

    rows = lax.broadcasted_iota(jnp.int32, (tm, te), 0)
    for j in range(CONV_WIDTH // te):
        cols = slice(j * te, (j + 1) * te)

        def proj(p):
            w = win_ref[:, p * CONV_WIDTH + j * te:p * CONV_WIDTH + (j + 1) * te]
            return jnp.dot(h_ref[...], w, preferred_element_type=F32)

        v = proj(1) * proj(2)
        prev1 = carry_ref[1:2, cols]
        prev2 = carry_ref[0:1, cols]
        v1 = jnp.where(rows == 0, prev1, pltpu.roll(v, 1, 0))
        v2 = jnp.where(rows == 0, prev2,
                       jnp.where(rows == 1, prev1, pltpu.roll(v, 2, 0)))
        carry_ref[0:2, cols] = v[tm - 2:tm]
        wc = wc_ref[:, cols]
        conv = wc[0:1] * v2 + wc[1:2] * v1 + wc[2:3] * v
        y = proj(0) * conv * jax.nn.silu(proj(3))
        part = jnp.dot(y.astype(BF16), wout_ref[cols, :],
                       preferred_element_type=F32)
        if j == 0:
            o_ref[...] = x + part
        else:
            o_ref[...] += part


def _conv_layer(x2, norm_g, w_in, w_conv, w_out, layer, seq):
    t = x2.shape[0]
    tm, te = 2 * TOKEN_TILE, MXU_COLS
    kern = functools.partial(_conv_layer_kernel, tm=tm, te=te,
                             tiles_per_seq=seq // tm)
    return pl.pallas_call(
        kern,
        grid=(t // tm,),
        in_specs=[
            pl.BlockSpec((tm, D_MODEL), lambda i: (i, 0)),
            _resident((1, D_MODEL), layer),
            _resident((D_MODEL, 4 * CONV_WIDTH), layer),
            _resident((CONV_K, CONV_WIDTH), layer),
            _resident((CONV_WIDTH, D_MODEL), layer),
        ],
        out_specs=pl.BlockSpec((tm, D_MODEL), lambda i: (i, 0)),
        out_shape=jax.ShapeDtypeStruct((t, D_MODEL), F32),
        scratch_shapes=[pltpu.VMEM((tm, D_MODEL), BF16),
                        pltpu.VMEM((8, CONV_WIDTH), F32)],
        compiler_params=_params(),
        name="conv_layer",
    )(x2, norm_g, w_in, w_conv, w_out)


def _attn_proj_kernel(x_ref, g_ref, w_ref, qg_ref, kg_ref, bd_ref,
                      qkv0_ref, qkv1_ref, qkv2_ref, gate_ref, h_ref, slab_ref,
                      *, tm):
    h32 = _rms_rows(x_ref[...], g_ref[...])
    h_ref[0] = h32.astype(BF16)
    for j in range(LANE_TILES):
        slab_ref[j] = h32[:, j * LANES:(j + 1) * LANES]
    for g in range(1, N_GROUPS):
        r = DILATED_GROUPS[g][1]
        sub = tm // r
        for c in range(r):
            for j in range(LANE_TILES):
                piece = slab_ref[j, pl.ds(c, sub, stride=r), :]
                h_ref[g, c * sub:(c + 1) * sub, j * LANES:(j + 1) * LANES] = piece.astype(BF16)

    out_refs = (qkv0_ref, qkv1_ref, qkv2_ref)
    for g in range(N_GROUPS):
        r = DILATED_GROUPS[g][1]
        sub = tm // r
        for part in range(3):
            col0 = (g * 3 + part) * ATTN_WIDTH
            a = jnp.dot(h_ref[g], w_ref[:, col0:col0 + ATTN_WIDTH],
                        preferred_element_type=F32)
            gain_ref = (qg_ref, kg_ref, None)[part]
            for s in range(ATTN_WIDTH // MXU_COLS):
                sl = slice(s * MXU_COLS, (s + 1) * MXU_COLS)
                a_s = a[:, sl]
                if gain_ref is not None:
                    ss = jnp.dot((a_s * a_s).astype(BF16), bd_ref[...],
                                 preferred_element_type=F32)
                    a_s = a_s * lax.rsqrt(ss * (1.0 / HEAD_DIM) + EPS) * gain_ref[g:g + 1, sl]
                a_s = a_s.astype(BF16)
                for c in range(r):
                    out_refs[g][c, :, part * ATTN_WIDTH + s * MXU_COLS:
                                part * ATTN_WIDTH + (s + 1) * MXU_COLS] = a_s[c * sub:(c + 1) * sub]

    z = jnp.dot(h_ref[0], w_ref[:, QKV_COLS:QKV_COLS + ATTN_WIDTH],
                preferred_element_type=F32)
    gate_ref[...] = jax.nn.silu(z).astype(BF16)


def _dilated_block(r, sub, width, tiles_per_seq):
    return pl.BlockSpec((None, r, sub, width),
                        lambda i: (i // tiles_per_seq, 0, i % tiles_per_seq, 0))


def _attn_proj(x2, norm_g, w_in, q_gain, k_gain, layer, batch, seq):
    t = x2.shape[0]
    tm = TOKEN_TILE
    tiles_per_seq = seq // tm
    head_of_lane = jnp.arange(MXU_COLS) // HEAD_DIM
    block_diag = (head_of_lane[:, None] == head_of_lane[None, :]).astype(BF16)
    qg = jnp.tile(q_gain, (1, 1, N_HEADS)) * (HEAD_DIM ** -0.5 * LOG2E)
    kg = jnp.tile(k_gain, (1, 1, N_HEADS))
    qkv_specs, qkv_shapes = [], []
    for _, r in DILATED_GROUPS:
        qkv_specs.append(_dilated_block(r, tm // r, 3 * ATTN_WIDTH, tiles_per_seq))
        qkv_shapes.append(jax.ShapeDtypeStruct((batch, r, seq // r, 3 * ATTN_WIDTH), BF16))
    return pl.pallas_call(
        functools.partial(_attn_proj_kernel, tm=tm),
        grid=(t // tm,),
        in_specs=[
            pl.BlockSpec((tm, D_MODEL), lambda i: (i, 0)),
            _resident((1, D_MODEL), layer),
            _resident((D_MODEL, QKV_COLS + ATTN_WIDTH), layer),
            _resident((N_GROUPS, ATTN_WIDTH), layer),
            _resident((N_GROUPS, ATTN_WIDTH), layer),
            _resident((MXU_COLS, MXU_COLS)),
        ],
        out_specs=qkv_specs + [pl.BlockSpec((tm, ATTN_WIDTH), lambda i: (i, 0))],
        out_shape=qkv_shapes + [jax.ShapeDtypeStruct((t, ATTN_WIDTH), BF16)],
        scratch_shapes=[pltpu.VMEM((N_GROUPS, tm, D_MODEL), BF16),
                        pltpu.VMEM((LANE_TILES, tm, LANES), F32)],
        compiler_params=_params(),
        name="attn_proj",
    )(x2, norm_g, w_in, qg, kg, block_diag)


def _attn_group_kernel(q_ref, k_ref, v_ref, bias_ref, o_ref, stat_ref,
                       kprev_ref, vprev_ref, *, nq):
    n = pl.program_id(2)

    @pl.when(n == 0)
    def _():
        kprev_ref[...] = jnp.zeros_like(kprev_ref)
        vprev_ref[...] = jnp.zeros_like(vprev_ref)

    first = jnp.where(n == 0, 1, 0)
    lane = lax.broadcasted_iota(jnp.int32, (BLOCK, LANES), 1)
    low = lane < HEAD_DIM
    lane_row = lax.broadcasted_iota(jnp.int32, (1, LANES), 1)
    low_b = (lane_row < HEAD_DIM).astype(F32).astype(BF16)
    high_b = (lane_row >= HEAD_DIM).astype(F32).astype(BF16)
    ones = jnp.ones((2 * BLOCK, LANES), BF16)

    for qb in range(nq):
        rows = slice(qb * BLOCK, (qb + 1) * BLOCK)
        stat_tile = jnp.zeros((BLOCK, LANES), F32)
        for pr in range(N_PAIRS):
            cols = slice(pr * LANES, (pr + 1) * LANES)
            q = q_ref[rows, cols]
            q2 = jnp.concatenate([q * low_b, q * high_b], axis=0)
            if qb == 0:
                kcat = jnp.concatenate([kprev_ref[:, cols], k_ref[0:BLOCK, cols]], axis=0)
                vcat = jnp.concatenate([vprev_ref[:, cols], v_ref[0:BLOCK, cols]], axis=0)
                bidx = first
            else:
                krows = slice((qb - 1) * BLOCK, (qb + 1) * BLOCK)
                kcat = k_ref[krows, cols]
                vcat = v_ref[krows, cols]
                bidx = 0
            s = lax.dot_general(q2, kcat, (((1,), (1,)), ((), ())),
                                preferred_element_type=F32) + bias_ref[bidx, pr]
            m = jnp.max(s, axis=1, keepdims=True)
            p = jnp.exp2(s - m).astype(BF16)
            oe = jnp.dot(p, jnp.concatenate([vcat, ones], axis=1),
                         preferred_element_type=F32)
            o = jnp.where(low, oe[0:BLOCK, :LANES], oe[BLOCK:2 * BLOCK, :LANES])
            o_ref[rows, cols] = o.astype(BF16)
            l = oe[:, LANES:]
            for hh in range(HEADS_PER_TILE):
                head = HEADS_PER_TILE * pr + hh
                hrows = slice(hh * BLOCK, (hh + 1) * BLOCK)
                stat_tile = jnp.where(lane == head, m[hrows], stat_tile)
                stat_tile = jnp.where(lane == N_HEADS + head, l[hrows], stat_tile)
        stat_ref[rows, :] = stat_tile

    kprev_ref[...] = k_ref[(nq - 1) * BLOCK:nq * BLOCK, :]
    vprev_ref[...] = v_ref[(nq - 1) * BLOCK:nq * BLOCK, :]


def _group_bias(rel_bias, g, dilation, span):
    a = jnp.arange(BLOCK)[:, None]
    kk = jnp.arange(2 * BLOCK)[None, :]
    step = BLOCK + a - kk
    valid = (step >= 0) & (step <= span)
    dist = jnp.clip(step, 0, span) * dilation
    max_exact = N_BUCKETS // 2
    d = jnp.maximum(dist, 1).astype(F32)
    large = max_exact + (jnp.log(d / max_exact) / math.log(MAX_DISTANCE / max_exact)
                         * (N_BUCKETS - max_exact)).astype(jnp.int32)
    large = jnp.minimum(large, N_BUCKETS - 1)
    bucket = jnp.where(dist < max_exact, dist, large)
    tab = rel_bias[:, g * N_HEADS:(g + 1) * N_HEADS].astype(F32)
    onehot = (bucket[..., None] == jnp.arange(N_BUCKETS)).astype(F32)
    bias = jnp.einsum("akn,nh->hak", onehot, tab, precision=lax.Precision.HIGHEST) * LOG2E
    interior = jnp.where(valid[None], bias, MASKED)
    first = jnp.where((valid & (kk >= BLOCK))[None], bias, MASKED)
    both = jnp.stack([interior, first])
    return both.reshape(2, N_PAIRS, HEADS_PER_TILE * BLOCK, 2 * BLOCK)


def _attn_group(qkv, rel_bias, g):
    batch, dilation, sub_len, _ = qkv.shape
    span = DILATED_GROUPS[g][0] // dilation
    nq = min(4, sub_len // BLOCK)
    rows = nq * BLOCK
    bias = _group_bias(rel_bias, g, dilation, span)

    def part_spec(part):
        return pl.BlockSpec((None, None, rows, ATTN_WIDTH),
                            lambda b, c, n: (b, c, n, part))

    return pl.pallas_call(
        functools.partial(_attn_group_kernel, nq=nq),
        grid=(batch, dilation, sub_len // rows),
        in_specs=[part_spec(0), part_spec(1), part_spec(2),
                  _resident(bias.shape)],
        out_specs=[pl.BlockSpec((None, None, rows, ATTN_WIDTH), lambda b, c, n: (b, c, n, 0)),
                   pl.BlockSpec((None, None, rows, LANES), lambda b, c, n: (b, c, n, 0))],
        out_shape=[jax.ShapeDtypeStruct((batch, dilation, sub_len, ATTN_WIDTH), BF16),
                   jax.ShapeDtypeStruct((batch, dilation, sub_len, LANES), F32)],
        scratch_shapes=[pltpu.VMEM((BLOCK, ATTN_WIDTH), BF16),
                        pltpu.VMEM((BLOCK, ATTN_WIDTH), BF16)],
        compiler_params=_params(3),
        name=f"attn_group{g}",
    )(qkv, qkv, qkv, bias)


def _merge_out_kernel(x_ref, o0_ref, o1_ref, o2_ref, s0_ref, s1_ref, s2_ref,
                      gate_ref, expand_ref, w_ref, out_ref,
                      snat_ref, onat_ref, y_ref, *, tm):
    o_refs = (o0_ref, o1_ref, o2_ref)
    s_refs = (s0_ref, s1_ref, s2_ref)
    assert DILATED_GROUPS[0][1] == 1
    for g in range(1, N_GROUPS):
        r = DILATED_GROUPS[g][1]
        sub = tm // r
        for c in range(r):
            snat_ref[g - 1, pl.ds(c, sub, stride=r), :] = s_refs[g][c]
            for j in range(LANE_TILES):
                onat_ref[g - 1, j, pl.ds(c, sub, stride=r), :] = (
                    o_refs[g][c, :, j * LANES:(j + 1) * LANES].astype(F32))

    def o_tile(g, j):
        if g == 0:
            return o0_ref[0, :, j * LANES:(j + 1) * LANES].astype(F32)
        return onat_ref[g - 1, j]

    stats = [s0_ref[0]] + [snat_ref[g - 1] for g in range(1, N_GROUPS)]
    m = jnp.maximum(jnp.maximum(stats[0], stats[1]), stats[2])
    es = [jnp.exp2(st - m) for st in stats]
    den = sum(e * pltpu.roll(st, LANES - N_HEADS, 1) for e, st in zip(es, stats))
    head_lane = lax.broadcasted_iota(jnp.int32, (tm, LANES), 1) < N_HEADS
    inv = 1.0 / jnp.where(head_lane, den, 1.0)
    spreads = []
    for e in es:
        alpha = jnp.where(head_lane, e * inv, 0.0)
        hi = alpha.astype(BF16)
        lo = (alpha - hi.astype(F32)).astype(BF16)
        spreads.append(jnp.dot(jnp.concatenate([hi, lo], axis=1), expand_ref[...],
                               preferred_element_type=F32))
    for j in range(LANE_TILES):
        cols = slice(j * LANES, (j + 1) * LANES)
        o = sum(spreads[g][:, cols] * o_tile(g, j) for g in range(N_GROUPS))
        y_ref[:, cols] = (o * gate_ref[:, cols].astype(F32)).astype(BF16)
    out_ref[...] = x_ref[...] + jnp.dot(y_ref[...], w_ref[...],
                                        preferred_element_type=F32)


def _merge_out(x2, outs, stats, gate, w_out, layer, seq):
    t = x2.shape[0]
    tm = TOKEN_TILE
    tiles_per_seq = seq // tm
    lane_head = jnp.arange(2 * LANES) % LANES
    expand = (lane_head[:, None] == (jnp.arange(ATTN_WIDTH) // HEAD_DIM)[None, :]).astype(BF16)
    row = lambda w: pl.BlockSpec((tm, w), lambda i: (i, 0))
    o_specs = [_dilated_block(r, tm // r, ATTN_WIDTH, tiles_per_seq) for _, r in DILATED_GROUPS]
    l_specs = [_dilated_block(r, tm // r, LANES, tiles_per_seq) for _, r in DILATED_GROUPS]
    return pl.pallas_call(
        functools.partial(_merge_out_kernel, tm=tm),
        grid=(t // tm,),
        in_specs=[row(D_MODEL)] + o_specs + l_specs
                 + [row(ATTN_WIDTH), _resident(expand.shape),
                    _resident((ATTN_WIDTH, D_MODEL), layer)],
        out_specs=row(D_MODEL),
        out_shape=jax.ShapeDtypeStruct((t, D_MODEL), F32),
        scratch_shapes=[pltpu.VMEM((N_GROUPS - 1, tm, LANES), F32),
                        pltpu.VMEM((N_GROUPS - 1, LANE_TILES, tm, LANES), F32),
                        pltpu.VMEM((tm, ATTN_WIDTH), BF16)],
        compiler_params=_params(),
        name="merge_out",
    )(x2, *outs, *stats, gate, expand, w_out)


def _attn_layer(x2, norm_g, w_in, q_gain, k_gain, w_out, rel_bias, layer, batch, seq):
    *qkvs, gate = _attn_proj(x2, norm_g, w_in, q_gain, k_gain, layer, batch, seq)
    outs, stats = [], []
    for g in range(N_GROUPS):
        o, st = _attn_group(qkvs[g], rel_bias, g)
        outs.append(o)
        stats.append(st)
    return _merge_out(x2, outs, stats, gate, w_out, layer, seq)


def kernel(x, conv_norm, conv_w_in, conv_w, conv_w_out, attn_norm, attn_w_in,
           attn_q_gain, attn_k_gain, attn_w_out, rel_bias):
    batch, seq, d = x.shape
    assert d == D_MODEL and seq % TOKEN_TILE == 0
    assert all(TOKEN_TILE % (8 * r) == 0 and seq % (BLOCK * r) == 0
               for _, r in DILATED_GROUPS)
    depth = conv_norm.shape[0] + attn_norm.shape[0]
    conv_p = (conv_norm[:, None, :], conv_w_in.astype(BF16), conv_w,
              conv_w_out.astype(BF16))
    attn_p = (attn_norm[:, None, :], attn_w_in.astype(BF16), attn_q_gain,
              attn_k_gain, attn_w_out.astype(BF16))
    x2 = x.reshape(batch * seq, d)
    for i in range(depth):
        j = i // 2
        if i % 2 == 0:
            x2 = _conv_layer(x2, *conv_p, j, seq)
        else:
            x2 = _attn_layer(x2, *attn_p, rel_bias, j, batch, seq)
    return x2.reshape(batch, seq, d)
```

```python
import functools
import math

import jax
import jax.numpy as jnp
from jax import lax
from jax.experimental import pallas as pl
from jax.experimental.pallas import tpu as pltpu

D_MODEL = 1024
CONV_WIDTH = 2048
CONV_K = 3
HEAD_DIM = 64
N_HEADS = 16
ATTN_WIDTH = N_HEADS * HEAD_DIM
DILATED_GROUPS = ((128, 1), (512, 4), (2048, 16))
N_GROUPS = len(DILATED_GROUPS)
QKV_COLS = N_GROUPS * 3 * ATTN_WIDTH
BLOCK = 128
N_BUCKETS = 32
MAX_DISTANCE = 2048
EPS = 1e-6
MASKED = -1e30

LANES = 128
MXU_COLS = 256
HEADS_PER_TILE = LANES // HEAD_DIM
N_PAIRS = N_HEADS // HEADS_PER_TILE
LANE_TILES = ATTN_WIDTH // LANES
VMEM_LIMIT = 56 * 1024 * 1024
TOKEN_TILE = 512
FAST_STRIDE = 4
LOG2E = math.log2(math.e)

F32 = jnp.float32
BF16 = jnp.bfloat16


def _resident(shape, layer=None):
    if layer is None:
        return pl.BlockSpec(shape, lambda *_: (0,) * len(shape),
                            pipeline_mode=pl.Buffered(1))
    return pl.BlockSpec((None,) + tuple(shape), lambda *_: (layer,) + (0,) * len(shape),
                        pipeline_mode=pl.Buffered(1))


def _params(n_axes=1):
    return pltpu.CompilerParams(
        dimension_semantics=("arbitrary",) * n_axes, vmem_limit_bytes=VMEM_LIMIT)


def _rms_rows(x, gain):
    ms = jnp.mean(x * x, axis=-1, keepdims=True)
    return x * lax.rsqrt(ms + EPS) * gain


def _conv_layer_kernel(x_ref, g_ref, win_ref, wc_ref, wout_ref, o_ref,
                       h_ref, carry_ref, *, tm, te, tiles_per_seq):
    i = pl.program_id(0)
    x = x_ref[...]
    h_ref[...] = _rms_rows(x, g_ref[...]).astype(BF16)

    @pl.when(i % tiles_per_seq == 0)
    def _():
        carry_ref[...] = jnp.zeros_like(carry_ref)

    rows = lax.broadcasted_iota(jnp.int32, (tm, te), 0)
    for j in range(CONV_WIDTH // te):
        cols = slice(j * te, (j + 1) * te)

        def proj(p):
            w = win_ref[:, p * CONV_WIDTH + j * te:p * CONV_WIDTH + (j + 1) * te]
            return jnp.dot(h_ref[...], w, preferred_element_type=F32)

        v = proj(1) * proj(2)
        prev1 = carry_ref[1:2, cols]
        prev2 = carry_ref[0:1, cols]
        v1 = jnp.where(rows == 0, prev1, pltpu.roll(v, 1, 0))
        v2 = jnp.where(rows == 0, prev2,
                       jnp.where(rows == 1, prev1, pltpu.roll(v, 2, 0)))
        carry_ref[0:2, cols] = v[tm - 2:tm]
        wc = wc_ref[:, cols]
        conv = wc[0:1] * v2 + wc[1:2] * v1 + wc[2:3] * v
        y = proj(0) * conv * jax.nn.silu(proj(3))
        part = jnp.dot(y.astype(BF16), wout_ref[cols, :],
                       preferred_element_type=F32)
        if j == 0:
            o_ref[...] = x + part
        else:
            o_ref[...] += part


def _conv_layer(x2, norm_g, w_in, w_conv, w_out, layer, seq):
    t = x2.shape[0]
    tm, te = 2 * TOKEN_TILE, MXU_COLS
    kern = functools.partial(_conv_layer_kernel, tm=tm, te=te,
                             tiles_per_seq=seq // tm)
    return pl.pallas_call(
        kern,
        grid=(t // tm,),
        in_specs=[
            pl.BlockSpec((tm, D_MODEL), lambda i: (i, 0)),
            _resident((1, D_MODEL), layer),
            _resident((D_MODEL, 4 * CONV_WIDTH), layer),
            _resident((CONV_K, CONV_WIDTH), layer),
            _resident((CONV_WIDTH, D_MODEL), layer),
        ],
        out_specs=pl.BlockSpec((tm, D_MODEL), lambda i: (i, 0)),
        out_shape=jax.ShapeDtypeStruct((t, D_MODEL), F32),
        scratch_shapes=[pltpu.VMEM((tm, D_MODEL), BF16),
                        pltpu.VMEM((8, CONV_WIDTH), F32)],
        compiler_params=_params(),
        name="conv_layer",
    )(x2, norm_g, w_in, w_conv, w_out)


def _attn_proj_kernel(x_ref, g_ref, w_ref, qg_ref, kg_ref,
                      qkv0_ref, qkv1_ref, qkv2_ref, gate_ref, h_ref, slab_ref,
                      *, tm):
    h32 = _rms_rows(x_ref[...], g_ref[...])
    h_ref[0] = h32.astype(BF16)
    for j in range(LANE_TILES):
        slab_ref[j] = h32[:, j * LANES:(j + 1) * LANES]
    for g in range(1, N_GROUPS):
        r = DILATED_GROUPS[g][1]
        sub = tm // r
        for c in range(r):
            for j in range(LANE_TILES):
                piece = slab_ref[j, pl.ds(c, sub, stride=r), :]
                h_ref[g, c * sub:(c + 1) * sub, j * LANES:(j + 1) * LANES] = piece.astype(BF16)

    out_refs = (qkv0_ref, qkv1_ref, qkv2_ref)
    low = lax.broadcasted_iota(jnp.int32, (tm, LANES), 1) < HEAD_DIM
    for g in range(N_GROUPS):
        r = DILATED_GROUPS[g][1]
        sub = tm // r
        for part in range(3):
            col0 = (g * 3 + part) * ATTN_WIDTH
            a = jnp.dot(h_ref[g], w_ref[:, col0:col0 + ATTN_WIDTH],
                        preferred_element_type=F32)
            gain_ref = (qg_ref, kg_ref, None)[part]
            for s in range(LANE_TILES):
                sl = slice(s * LANES, (s + 1) * LANES)
                a_s = a[:, sl]
                if gain_ref is not None:
                    sq = a_s * a_s
                    ss_low = jnp.sum(jnp.where(low, sq, 0.0), axis=-1, keepdims=True)
                    ss_high = jnp.sum(jnp.where(low, 0.0, sq), axis=-1, keepdims=True)
                    ss = jnp.where(low, ss_low, ss_high)
                    a_s = a_s * lax.rsqrt(ss * (1.0 / HEAD_DIM) + EPS) * gain_ref[g:g + 1, sl]
                a_s = a_s.astype(BF16)
                for c in range(r):
                    out_refs[g][c, :, part * ATTN_WIDTH + s * LANES:
                                part * ATTN_WIDTH + (s + 1) * LANES] = a_s[c * sub:(c + 1) * sub]

    z = jnp.dot(h_ref[0], w_ref[:, QKV_COLS:QKV_COLS + ATTN_WIDTH],
                preferred_element_type=F32)
    gate_ref[...] = jax.nn.silu(z).astype(BF16)


def _dilated_block(r, sub, width, tiles_per_seq):
    return pl.BlockSpec((None, r, sub, width),
                        lambda i: (i // tiles_per_seq, 0, i % tiles_per_seq, 0))


def _attn_proj(x2, norm_g, w_in, q_gain, k_gain, layer, batch, seq):
    t = x2.shape[0]
    tm = TOKEN_TILE
    tiles_per_seq = seq // tm
    qg = jnp.tile(q_gain, (1, 1, N_HEADS)) * (HEAD_DIM ** -0.5 * LOG2E)
    kg = jnp.tile(k_gain, (1, 1, N_HEADS))
    qkv_specs, qkv_shapes = [], []
    for _, r in DILATED_GROUPS:
        qkv_specs.append(_dilated_block(r, tm // r, 3 * ATTN_WIDTH, tiles_per_seq))
        qkv_shapes.append(jax.ShapeDtypeStruct((batch, r, seq // r, 3 * ATTN_WIDTH), BF16))
    return pl.pallas_call(
        functools.partial(_attn_proj_kernel, tm=tm),
        grid=(t // tm,),
        in_specs=[
            pl.BlockSpec((tm, D_MODEL), lambda i: (i, 0)),
            _resident((1, D_MODEL), layer),
            _resident((D_MODEL, QKV_COLS + ATTN_WIDTH), layer),
            _resident((N_GROUPS, ATTN_WIDTH), layer),
            _resident((N_GROUPS, ATTN_WIDTH), layer),
        ],
        out_specs=qkv_specs + [pl.BlockSpec((tm, ATTN_WIDTH), lambda i: (i, 0))],
        out_shape=qkv_shapes + [jax.ShapeDtypeStruct((t, ATTN_WIDTH), BF16)],
        scratch_shapes=[pltpu.VMEM((N_GROUPS, tm, D_MODEL), BF16),
                        pltpu.VMEM((LANE_TILES, tm, LANES), F32)],
        compiler_params=_params(),
        name="attn_proj",
    )(x2, norm_g, w_in, qg, kg)


def _attn_group_kernel(q_ref, k_ref, v_ref, bias_ref, o_ref, stat_ref,
                       kprev_ref, vprev_ref, *, nq):
    n = pl.program_id(2)

    @pl.when(n == 0)
    def _():
        kprev_ref[...] = jnp.zeros_like(kprev_ref)
        vprev_ref[...] = jnp.zeros_like(vprev_ref)

    first = jnp.where(n == 0, 1, 0)
    lane = lax.broadcasted_iota(jnp.int32, (BLOCK, LANES), 1)
    low = lane < HEAD_DIM
    lane_row = lax.broadcasted_iota(jnp.int32, (1, LANES), 1)
    low_b = (lane_row < HEAD_DIM).astype(F32).astype(BF16)
    high_b = (lane_row >= HEAD_DIM).astype(F32).astype(BF16)
    ones = jnp.ones((2 * BLOCK, LANES), BF16)

    for qb in range(nq):
        rows = slice(qb * BLOCK, (qb + 1) * BLOCK)
        stat_tile = jnp.zeros((BLOCK, LANES), F32)
        for pr in range(N_PAIRS):
            cols = slice(pr * LANES, (pr + 1) * LANES)
            q = q_ref[rows, cols]
            q2 = jnp.concatenate([q * low_b, q * high_b], axis=0)
            if qb == 0:
                kcat = jnp.concatenate([kprev_ref[:, cols], k_ref[0:BLOCK, cols]], axis=0)
                vcat = jnp.concatenate([vprev_ref[:, cols], v_ref[0:BLOCK, cols]], axis=0)
                bidx = first
            else:
                krows = slice((qb - 1) * BLOCK, (qb + 1) * BLOCK)
                kcat = k_ref[krows, cols]
                vcat = v_ref[krows, cols]
                bidx = 0
            s = lax.dot_general(q2, kcat, (((1,), (1,)), ((), ())),
                                preferred_element_type=F32) + bias_ref[bidx, pr]
            m = jnp.max(s, axis=1, keepdims=True)
            p = jnp.exp2(s - m).astype(BF16)
            oe = jnp.dot(p, jnp.concatenate([vcat, ones], axis=1),
                         preferred_element_type=F32)
            o = jnp.where(low, oe[0:BLOCK, :LANES], oe[BLOCK:2 * BLOCK, :LANES])
            o_ref[rows, cols] = o.astype(BF16)
            l = oe[:, LANES:]
            for hh in range(HEADS_PER_TILE):
                head = HEADS_PER_TILE * pr + hh
                hrows = slice(hh * BLOCK, (hh + 1) * BLOCK)
                stat_tile = jnp.where(lane == head, m[hrows], stat_tile)
                stat_tile = jnp.where(lane == N_HEADS + head, l[hrows], stat_tile)
        stat_ref[rows, :] = stat_tile

    kprev_ref[...] = k_ref[(nq - 1) * BLOCK:nq * BLOCK, :]
    vprev_ref[...] = v_ref[(nq - 1) * BLOCK:nq * BLOCK, :]


def _group_bias(rel_bias, g, dilation, span):
    a = jnp.arange(BLOCK)[:, None]
    kk = jnp.arange(2 * BLOCK)[None, :]
    step = BLOCK + a - kk
    valid = (step >= 0) & (step <= span)
    dist = jnp.clip(step, 0, span) * dilation
    max_exact = N_BUCKETS // 2
    d = jnp.maximum(dist, 1).astype(F32)
    large = max_exact + (jnp.log(d / max_exact) / math.log(MAX_DISTANCE / max_exact)
                         * (N_BUCKETS - max_exact)).astype(jnp.int32)
    large = jnp.minimum(large, N_BUCKETS - 1)
    bucket = jnp.where(dist < max_exact, dist, large)
    tab = rel_bias[:, g * N_HEADS:(g + 1) * N_HEADS].astype(F32)
    onehot = (bucket[..., None] == jnp.arange(N_BUCKETS)).astype(F32)
    bias = jnp.einsum("akn,nh->hak", onehot, tab, precision=lax.Precision.HIGHEST) * LOG2E
    interior = jnp.where(valid[None], bias, MASKED)
    first = jnp.where((valid & (kk >= BLOCK))[None], bias, MASKED)
    both = jnp.stack([interior, first])
    return both.reshape(2, N_PAIRS, HEADS_PER_TILE * BLOCK, 2 * BLOCK)


def _attn_group(qkv, rel_bias, g):
    batch, dilation, sub_len, _ = qkv.shape
    span = DILATED_GROUPS[g][0] // dilation
    nq = min(4, sub_len // BLOCK)
    rows = nq * BLOCK
    bias = _group_bias(rel_bias, g, dilation, span)

    def part_spec(part):
        return pl.BlockSpec((None, None, rows, ATTN_WIDTH),
                            lambda b, c, n: (b, c, n, part))

    return pl.pallas_call(
        functools.partial(_attn_group_kernel, nq=nq),
        grid=(batch, dilation, sub_len // rows),
        in_specs=[part_spec(0), part_spec(1), part_spec(2),
                  _resident(bias.shape)],
        out_specs=[pl.BlockSpec((None, None, rows, ATTN_WIDTH), lambda b, c, n: (b, c, n, 0)),
                   pl.BlockSpec((None, None, rows, LANES), lambda b, c, n: (b, c, n, 0))],
        out_shape=[jax.ShapeDtypeStruct((batch, dilation, sub_len, ATTN_WIDTH), BF16),
                   jax.ShapeDtypeStruct((batch, dilation, sub_len, LANES), F32)],
        scratch_shapes=[pltpu.VMEM((BLOCK, ATTN_WIDTH), BF16),
                        pltpu.VMEM((BLOCK, ATTN_WIDTH), BF16)],
        compiler_params=_params(3),
        name=f"attn_group{g}",
    )(qkv, qkv, qkv, bias)


def _merge_out_kernel(x_ref, o0_ref, o1_ref, o2_ref, s0_ref, s1_ref, s2_ref,
                      gate_ref, expand_ref, w_ref, out_ref,
                      snat_ref, onat_ref, stage_ref, y_ref, *, tm):
    o_refs = (o0_ref, o1_ref, o2_ref)
    s_refs = (s0_ref, s1_ref, s2_ref)

    def to_natural(dst, piece, r):
        sub = tm // r
        if r <= FAST_STRIDE:
            for c in range(r):
                dst[pl.ds(c, sub, stride=r), :] = piece(c)
            return
        outer = r // FAST_STRIDE
        for c0 in range(FAST_STRIDE):
            for c1 in range(outer):
                stage_ref[c0, pl.ds(c1, sub, stride=outer), :] = piece(c1 * FAST_STRIDE + c0)
            dst[pl.ds(c0, sub * outer, stride=FAST_STRIDE), :] = stage_ref[c0]

    assert DILATED_GROUPS[0][1] == 1
    for g in range(1, N_GROUPS):
        r = DILATED_GROUPS[g][1]
        to_natural(snat_ref.at[g - 1], lambda c: s_refs[g][c], r)
        for j in range(LANE_TILES):
            to_natural(onat_ref.at[g - 1, j],
                       lambda c: o_refs[g][c, :, j * LANES:(j + 1) * LANES].astype(F32), r)

    def o_tile(g, j):
        if g == 0:
            return o0_ref[0, :, j * LANES:(j + 1) * LANES].astype(F32)
        return onat_ref[g - 1, j]

    stats = [s0_ref[0]] + [snat_ref[g - 1] for g in range(1, N_GROUPS)]
    m = jnp.maximum(jnp.maximum(stats[0], stats[1]), stats[2])
    es = [jnp.exp2(st - m) for st in stats]
    den = sum(e * pltpu.roll(st, LANES - N_HEADS, 1) for e, st in zip(es, stats))
    head_lane = lax.broadcasted_iota(jnp.int32, (tm, LANES), 1) < N_HEADS
    inv = 1.0 / jnp.where(head_lane, den, 1.0)
    spreads = []
    for e in es:
        alpha = jnp.where(head_lane, e * inv, 0.0)
        hi = alpha.astype(BF16)
        lo = (alpha - hi.astype(F32)).astype(BF16)
        spreads.append(jnp.dot(jnp.concatenate([hi, lo], axis=1), expand_ref[...],
                               preferred_element_type=F32))
    for j in range(LANE_TILES):
        cols = slice(j * LANES, (j + 1) * LANES)
        o = sum(spreads[g][:, cols] * o_tile(g, j) for g in range(N_GROUPS))
        y_ref[:, cols] = (o * gate_ref[:, cols].astype(F32)).astype(BF16)
    out_ref[...] = x_ref[...] + jnp.dot(y_ref[...], w_ref[...],
                                        preferred_element_type=F32)


def _merge_out(x2, outs, stats, gate, w_out, layer, seq):
    t = x2.shape[0]
    tm = TOKEN_TILE
    tiles_per_seq = seq // tm
    lane_head = jnp.arange(2 * LANES) % LANES
    expand = (lane_head[:, None] == (jnp.arange(ATTN_WIDTH) // HEAD_DIM)[None, :]).astype(BF16)
    row = lambda w: pl.BlockSpec((tm, w), lambda i: (i, 0))
    o_specs = [_dilated_block(r, tm // r, ATTN_WIDTH, tiles_per_seq) for _, r in DILATED_GROUPS]
    l_specs = [_dilated_block(r, tm // r, LANES, tiles_per_seq) for _, r in DILATED_GROUPS]
    return pl.pallas_call(
        functools.partial(_merge_out_kernel, tm=tm),
        grid=(t // tm,),
        in_specs=[row(D_MODEL)] + o_specs + l_specs
                 + [row(ATTN_WIDTH), _resident(expand.shape),
                    _resident((ATTN_WIDTH, D_MODEL), layer)],
        out_specs=row(D_MODEL),
        out_shape=jax.ShapeDtypeStruct((t, D_MODEL), F32),
        scratch_shapes=[pltpu.VMEM((N_GROUPS - 1, tm, LANES), F32),
                        pltpu.VMEM((N_GROUPS - 1, LANE_TILES, tm, LANES), F32),
                        pltpu.VMEM((FAST_STRIDE, tm // FAST_STRIDE, LANES), F32),
                        pltpu.VMEM((tm, ATTN_WIDTH), BF16)],
        compiler_params=_params(),
        name="merge_out",
    )(x2, *outs, *stats, gate, expand, w_out)


def _attn_layer(x2, norm_g, w_in, q_gain, k_gain, w_out, rel_bias, layer, batch, seq):
    *qkvs, gate = _attn_proj(x2, norm_g, w_in, q_gain, k_gain, layer, batch, seq)
    outs, stats = [], []
    for g in range(N_GROUPS):
        o, st = _attn_group(qkvs[g], rel_bias, g)
        outs.append(o)
        stats.append(st)
    return _merge_out(x2, outs, stats, gate, w_out, layer, seq)


def kernel(x, conv_norm, conv_w_in, conv_w, conv_w_out, attn_norm, attn_w_in,
           attn_q_gain, attn_k_gain, attn_w_out, rel_bias):
    batch, seq, d = x.shape
    assert d == D_MODEL and seq % TOKEN_TILE == 0
    assert all(TOKEN_TILE % (8 * r) == 0 and seq % (BLOCK * r) == 0
               for _, r in DILATED_GROUPS)
    depth = conv_norm.shape[0] + attn_norm.shape[0]
    conv_p = (conv_norm[:, None, :], conv_w_in.astype(BF16), conv_w,
              conv_w_out.astype(BF16))
    attn_p = (attn_norm[:, None, :], attn_w_in.astype(BF16), attn_q_gain,
              attn_k_gain, attn_w_out.astype(BF16))
    x2 = x.reshape(batch * seq, d)
    for i in range(depth):
        j = i // 2
        if i % 2 == 0:
            x2 = _conv_layer(x2, *conv_p, j, seq)
        else:
            x2 = _attn_layer(x2, *attn_p, rel_bias, j, batch, seq)
    return x2.reshape(batch, seq, d)
```

```python
import functools
import math

import jax
import jax.numpy as jnp
from jax import lax
from jax.experimental import pallas as pl
from jax.experimental.pallas import tpu as pltpu

D_MODEL = 1024
CONV_WIDTH = 2048
CONV_K = 3
HEAD_DIM = 64
N_HEADS = 16
ATTN_WIDTH = N_HEADS * HEAD_DIM
DILATED_GROUPS = ((128, 1), (512, 4), (2048, 16))
N_GROUPS = len(DILATED_GROUPS)
QKV_COLS = N_GROUPS * 3 * ATTN_WIDTH
BLOCK = 128
N_BUCKETS = 32
MAX_DISTANCE = 2048
EPS = 1e-6
MASKED = -1e30

LANES = 128
MXU_COLS = 256
HEADS_PER_TILE = LANES // HEAD_DIM
N_PAIRS = N_HEADS // HEADS_PER_TILE
LANE_TILES = ATTN_WIDTH // LANES
VMEM_LIMIT = 56 * 1024 * 1024
TOKEN_TILE = 512
ATTN_BLOCKS_PER_STEP = 16
FAST_STRIDE = 4
LOG2E = math.log2(math.e)

F32 = jnp.float32
BF16 = jnp.bfloat16


def _resident(shape, layer=None):
    if layer is None:
        return pl.BlockSpec(shape, lambda *_: (0,) * len(shape),
                            pipeline_mode=pl.Buffered(1))
    return pl.BlockSpec((None,) + tuple(shape), lambda *_: (layer,) + (0,) * len(shape),
                        pipeline_mode=pl.Buffered(1))


def _params(n_axes=1):
    return pltpu.CompilerParams(
        dimension_semantics=("arbitrary",) * n_axes, vmem_limit_bytes=VMEM_LIMIT)


def _rms_rows(x, gain):
    ms = jnp.mean(x * x, axis=-1, keepdims=True)
    return x * lax.rsqrt(ms + EPS) * gain


def _conv_layer_kernel(x_ref, g_ref, win_ref, wc_ref, wout_ref, o_ref,
                       h_ref, carry_ref, *, tm, te, tiles_per_seq):
    i = pl.program_id(0)
    x = x_ref[...]
    h_ref[...] = _rms_rows(x, g_ref[...]).astype(BF16)

    @pl.when(i % tiles_per_seq == 0)
    def _():
        carry_ref[...] = jnp.zeros_like(carry_ref)

    rows = lax.broadcasted_iota(jnp.int32, (tm, te), 0)
    for j in range(CONV_WIDTH // te):
        cols = slice(j * te, (j + 1) * te)

        def proj(p):
            w = win_ref[:, p * CONV_WIDTH + j * te:p * CONV_WIDTH + (j + 1) * te]
            return jnp.dot(h_ref[...], w, preferred_element_type=F32)

        v = proj(1) * proj(2)
        prev1 = carry_ref[1:2, cols]
        prev2 = carry_ref[0:1, cols]
        v1 = jnp.where(rows == 0, prev1, pltpu.roll(v, 1, 0))
        v2 = jnp.where(rows == 0, prev2,
                       jnp.where(rows == 1, prev1, pltpu.roll(v, 2, 0)))
        carry_ref[0:2, cols] = v[tm - 2:tm]
        wc = wc_ref[:, cols]
        conv = wc[0:1] * v2 + wc[1:2] * v1 + wc[2:3] * v
        y = proj(0) * conv * jax.nn.silu(proj(3))
        part = jnp.dot(y.astype(BF16), wout_ref[cols, :],
                       preferred_element_type=F32)
        if j == 0:
            o_ref[...] = x + part
        else:
            o_ref[...] += part


def _conv_layer(x2, norm_g, w_in, w_conv, w_out, layer, seq):
    t = x2.shape[0]
    tm, te = 2 * TOKEN_TILE, MXU_COLS
    kern = functools.partial(_conv_layer_kernel, tm=tm, te=te,
                             tiles_per_seq=seq // tm)
    return pl.pallas_call(
        kern,
        grid=(t // tm,),
        in_specs=[
            pl.BlockSpec((tm, D_MODEL), lambda i: (i, 0)),
            _resident((1, D_MODEL), layer),
            _resident((D_MODEL, 4 * CONV_WIDTH), layer),
            _resident((CONV_K, CONV_WIDTH), layer),
            _resident((CONV_WIDTH, D_MODEL), layer),
        ],
        out_specs=pl.BlockSpec((tm, D_MODEL), lambda i: (i, 0)),
        out_shape=jax.ShapeDtypeStruct((t, D_MODEL), F32),
        scratch_shapes=[pltpu.VMEM((tm, D_MODEL), BF16),
                        pltpu.VMEM((8, CONV_WIDTH), F32)],
        compiler_params=_params(),
        name="conv_layer",
    )(x2, norm_g, w_in, w_conv, w_out)


def _attn_proj_kernel(x_ref, g_ref, w_ref, qg_ref, kg_ref,
                      qkv0_ref, qkv1_ref, qkv2_ref, gate_ref, h_ref, slab_ref,
                      *, tm):
    h32 = _rms_rows(x_ref[...], g_ref[...])
    h_ref[0] = h32.astype(BF16)
    for j in range(LANE_TILES):
        slab_ref[j] = h32[:, j * LANES:(j + 1) * LANES]
    for g in range(1, N_GROUPS):
        r = DILATED_GROUPS[g][1]
        sub = tm // r
        for c in range(r):
            for j in range(LANE_TILES):
                piece = slab_ref[j, pl.ds(c, sub, stride=r), :]
                h_ref[g, c * sub:(c + 1) * sub, j * LANES:(j + 1) * LANES] = piece.astype(BF16)

    out_refs = (qkv0_ref, qkv1_ref, qkv2_ref)
    low = lax.broadcasted_iota(jnp.int32, (tm, LANES), 1) < HEAD_DIM
    for g in range(N_GROUPS):
        r = DILATED_GROUPS[g][1]
        sub = tm // r
        for part in range(3):
            col0 = (g * 3 + part) * ATTN_WIDTH
            a = jnp.dot(h_ref[g], w_ref[:, col0:col0 + ATTN_WIDTH],
                        preferred_element_type=F32)
            gain_ref = (qg_ref, kg_ref, None)[part]
            for s in range(LANE_TILES):
                sl = slice(s * LANES, (s + 1) * LANES)
                a_s = a[:, sl]
                if gain_ref is not None:
                    sq = a_s * a_s
                    ss_low = jnp.sum(jnp.where(low, sq, 0.0), axis=-1, keepdims=True)
                    ss_high = jnp.sum(jnp.where(low, 0.0, sq), axis=-1, keepdims=True)
                    ss = jnp.where(low, ss_low, ss_high)
                    a_s = a_s * lax.rsqrt(ss * (1.0 / HEAD_DIM) + EPS) * gain_ref[g:g + 1, sl]
                a_s = a_s.astype(BF16)
                for c in range(r):
                    out_refs[g][c, :, part * ATTN_WIDTH + s * LANES:
                                part * ATTN_WIDTH + (s + 1) * LANES] = a_s[c * sub:(c + 1) * sub]

    z = jnp.dot(h_ref[0], w_ref[:, QKV_COLS:QKV_COLS + ATTN_WIDTH],
                preferred_element_type=F32)
    gate_ref[...] = jax.nn.silu(z).astype(BF16)


def _dilated_block(r, sub, width, tiles_per_seq):
    return pl.BlockSpec((None, r, sub, width),
                        lambda i: (i // tiles_per_seq, 0, i % tiles_per_seq, 0))


def _attn_proj(x2, norm_g, w_in, q_gain, k_gain, layer, batch, seq):
    t = x2.shape[0]
    tm = TOKEN_TILE
    tiles_per_seq = seq // tm
    qg = jnp.tile(q_gain, (1, 1, N_HEADS)) * (HEAD_DIM ** -0.5 * LOG2E)
    kg = jnp.tile(k_gain, (1, 1, N_HEADS))
    qkv_specs, qkv_shapes = [], []
    for _, r in DILATED_GROUPS:
        qkv_specs.append(_dilated_block(r, tm // r, 3 * ATTN_WIDTH, tiles_per_seq))
        qkv_shapes.append(jax.ShapeDtypeStruct((batch, r, seq // r, 3 * ATTN_WIDTH), BF16))
    return pl.pallas_call(
        functools.partial(_attn_proj_kernel, tm=tm),
        grid=(t // tm,),
        in_specs=[
            pl.BlockSpec((tm, D_MODEL), lambda i: (i, 0)),
            _resident((1, D_MODEL), layer),
            _resident((D_MODEL, QKV_COLS + ATTN_WIDTH), layer),
            _resident((N_GROUPS, ATTN_WIDTH), layer),
            _resident((N_GROUPS, ATTN_WIDTH), layer),
        ],
        out_specs=qkv_specs + [pl.BlockSpec((tm, ATTN_WIDTH), lambda i: (i, 0))],
        out_shape=qkv_shapes + [jax.ShapeDtypeStruct((t, ATTN_WIDTH), BF16)],
        scratch_shapes=[pltpu.VMEM((N_GROUPS, tm, D_MODEL), BF16),
                        pltpu.VMEM((LANE_TILES, tm, LANES), F32)],
        compiler_params=_params(),
        name="attn_proj",
    )(x2, norm_g, w_in, qg, kg)


def _attn_group_kernel(q_ref, k_ref, v_ref, bias_ref, o_ref, stat_ref, *scratch,
                       ncls, nq, carry):
    if carry:
        kprev_ref, vprev_ref = scratch
        n = pl.program_id(2)

        @pl.when(n == 0)
        def _():
            kprev_ref[...] = jnp.zeros_like(kprev_ref)
            vprev_ref[...] = jnp.zeros_like(vprev_ref)

        first = jnp.where(n == 0, 1, 0)
    else:
        first = 1
    lane = lax.broadcasted_iota(jnp.int32, (BLOCK, LANES), 1)
    low = lane < HEAD_DIM
    lane_row = lax.broadcasted_iota(jnp.int32, (1, LANES), 1)
    low_b = (lane_row < HEAD_DIM).astype(F32).astype(BF16)
    high_b = (lane_row >= HEAD_DIM).astype(F32).astype(BF16)
    ones = jnp.ones((2 * BLOCK, LANES), BF16)
    no_prev = jnp.zeros((BLOCK, LANES), BF16)

    for ci in range(ncls):
        for qb in range(nq):
            rows = slice(qb * BLOCK, (qb + 1) * BLOCK)
            stat_tile = jnp.zeros((BLOCK, LANES), F32)
            for pr in range(N_PAIRS):
                cols = slice(pr * LANES, (pr + 1) * LANES)
                q = q_ref[ci, rows, cols]
                q2 = jnp.concatenate([q * low_b, q * high_b], axis=0)
                if qb == 0:
                    kp = kprev_ref[:, cols] if carry else no_prev
                    vp = vprev_ref[:, cols] if carry else no_prev
                    kcat = jnp.concatenate([kp, k_ref[ci, 0:BLOCK, cols]], axis=0)
                    vcat = jnp.concatenate([vp, v_ref[ci, 0:BLOCK, cols]], axis=0)
                    bidx = first
                else:
                    krows = slice((qb - 1) * BLOCK, (qb + 1) * BLOCK)
                    kcat = k_ref[ci, krows, cols]
                    vcat = v_ref[ci, krows, cols]
                    bidx = 0
                s = lax.dot_general(q2, kcat, (((1,), (1,)), ((), ())),
                                    preferred_element_type=F32) + bias_ref[bidx, pr]
                m = jnp.max(s, axis=1, keepdims=True)
                p = jnp.exp2(s - m).astype(BF16)
                oe = jnp.dot(p, jnp.concatenate([vcat, ones], axis=1),
                             preferred_element_type=F32)
                o = jnp.where(low, oe[0:BLOCK, :LANES], oe[BLOCK:2 * BLOCK, :LANES])
                o_ref[ci, rows, cols] = o.astype(BF16)
                l = oe[:, LANES:]
                for hh in range(HEADS_PER_TILE):
                    head = HEADS_PER_TILE * pr + hh
                    hrows = slice(hh * BLOCK, (hh + 1) * BLOCK)
                    stat_tile = jnp.where(lane == head, m[hrows], stat_tile)
                    stat_tile = jnp.where(lane == N_HEADS + head, l[hrows], stat_tile)
            stat_ref[ci, rows, :] = stat_tile

    if carry:
        kprev_ref[...] = k_ref[0, (nq - 1) * BLOCK:nq * BLOCK, :]
        vprev_ref[...] = v_ref[0, (nq - 1) * BLOCK:nq * BLOCK, :]


def _group_bias(rel_bias, g, dilation, span):
    a = jnp.arange(BLOCK)[:, None]
    kk = jnp.arange(2 * BLOCK)[None, :]
    step = BLOCK + a - kk
    valid = (step >= 0) & (step <= span)
    dist = jnp.clip(step, 0, span) * dilation
    max_exact = N_BUCKETS // 2
    d = jnp.maximum(dist, 1).astype(F32)
    large = max_exact + (jnp.log(d / max_exact) / math.log(MAX_DISTANCE / max_exact)
                         * (N_BUCKETS - max_exact)).astype(jnp.int32)
    large = jnp.minimum(large, N_BUCKETS - 1)
    bucket = jnp.where(dist < max_exact, dist, large)
    tab = rel_bias[:, g * N_HEADS:(g + 1) * N_HEADS].astype(F32)
    onehot = (bucket[..., None] == jnp.arange(N_BUCKETS)).astype(F32)
    bias = jnp.einsum("akn,nh->hak", onehot, tab, precision=lax.Precision.HIGHEST) * LOG2E
    interior = jnp.where(valid[None], bias, MASKED)
    first = jnp.where((valid & (kk >= BLOCK))[None], bias, MASKED)
    both = jnp.stack([interior, first])
    return both.reshape(2, N_PAIRS, HEADS_PER_TILE * BLOCK, 2 * BLOCK)


def _attn_group(qkv, rel_bias, g):
    batch, dilation, sub_len, _ = qkv.shape
    span = DILATED_GROUPS[g][0] // dilation
    nq = min(ATTN_BLOCKS_PER_STEP, sub_len // BLOCK)
    rows = nq * BLOCK
    carry = rows < sub_len
    ncls = 1 if carry else min(dilation, ATTN_BLOCKS_PER_STEP // nq)
    bias = _group_bias(rel_bias, g, dilation, span)

    def spec(width, col):
        return pl.BlockSpec((None, ncls, rows, width), lambda b, c, n: (b, c, n, col))

    scratch = [pltpu.VMEM((BLOCK, ATTN_WIDTH), BF16)] * 2 if carry else []
    return pl.pallas_call(
        functools.partial(_attn_group_kernel, ncls=ncls, nq=nq, carry=carry),
        grid=(batch, dilation // ncls, sub_len // rows),
        in_specs=[spec(ATTN_WIDTH, 0), spec(ATTN_WIDTH, 1), spec(ATTN_WIDTH, 2),
                  _resident(bias.shape)],
        out_specs=[spec(ATTN_WIDTH, 0), spec(LANES, 0)],
        out_shape=[jax.ShapeDtypeStruct((batch, dilation, sub_len, ATTN_WIDTH), BF16),
                   jax.ShapeDtypeStruct((batch, dilation, sub_len, LANES), F32)],
        scratch_shapes=scratch,
        compiler_params=_params(3),
        name=f"attn_group{g}",
    )(qkv, qkv, qkv, bias)


def _merge_out_kernel(x_ref, o0_ref, o1_ref, o2_ref, s0_ref, s1_ref, s2_ref,
                      gate_ref, expand_ref, w_ref, out_ref,
                      snat_ref, onat_ref, stage_ref, y_ref, *, tm):
    o_refs = (o0_ref, o1_ref, o2_ref)
    s_refs = (s0_ref, s1_ref, s2_ref)

    def to_natural(dst, piece, r):
        sub = tm // r
        if r <= FAST_STRIDE:
            for c in range(r):
                dst[pl.ds(c, sub, stride=r), :] = piece(c)
            return
        outer = r // FAST_STRIDE
        for c0 in range(FAST_STRIDE):
            for c1 in range(outer):
                stage_ref[c0, pl.ds(c1, sub, stride=outer), :] = piece(c1 * FAST_STRIDE + c0)
            dst[pl.ds(c0, sub * outer, stride=FAST_STRIDE), :] = stage_ref[c0]

    assert DILATED_GROUPS[0][1] == 1
    for g in range(1, N_GROUPS):
        r = DILATED_GROUPS[g][1]
        to_natural(snat_ref.at[g - 1], lambda c: s_refs[g][c], r)
        for j in range(LANE_TILES):
            to_natural(onat_ref.at[g - 1, j],
                       lambda c: o_refs[g][c, :, j * LANES:(j + 1) * LANES].astype(F32), r)

    def o_tile(g, j):
        if g == 0:
            return o0_ref[0, :, j * LANES:(j + 1) * LANES].astype(F32)
        return onat_ref[g - 1, j]

    stats = [s0_ref[0]] + [snat_ref[g - 1] for g in range(1, N_GROUPS)]
    m = jnp.maximum(jnp.maximum(stats[0], stats[1]), stats[2])
    es = [jnp.exp2(st - m) for st in stats]
    den = sum(e * pltpu.roll(st, LANES - N_HEADS, 1) for e, st in zip(es, stats))
    head_lane = lax.broadcasted_iota(jnp.int32, (tm, LANES), 1) < N_HEADS
    inv = 1.0 / jnp.where(head_lane, den, 1.0)
    spreads = []
    for e in es:
        alpha = jnp.where(head_lane, e * inv, 0.0)
        hi = alpha.astype(BF16)
        lo = (alpha - hi.astype(F32)).astype(BF16)
        spreads.append(jnp.dot(jnp.concatenate([hi, lo], axis=1), expand_ref[...],
                               preferred_element_type=F32))
    for j in range(LANE_TILES):
        cols = slice(j * LANES, (j + 1) * LANES)
        o = sum(spreads[g][:, cols] * o_tile(g, j) for g in range(N_GROUPS))
        y_ref[:, cols] = (o * gate_ref[:, cols].astype(F32)).astype(BF16)
    out_ref[...] = x_ref[...] + jnp.dot(y_ref[...], w_ref[...],
                                        preferred_element_type=F32)


def _merge_out(x2, outs, stats, gate, w_out, layer, seq):
    t = x2.shape[0]
    tm = TOKEN_TILE
    tiles_per_seq = seq // tm
    lane_head = jnp.arange(2 * LANES) % LANES
    expand = (lane_head[:, None] == (jnp.arange(ATTN_WIDTH) // HEAD_DIM)[None, :]).astype(BF16)
    row = lambda w: pl.BlockSpec((tm, w), lambda i: (i, 0))
    o_specs = [_dilated_block(r, tm // r, ATTN_WIDTH, tiles_per_seq) for _, r in DILATED_GROUPS]
    l_specs = [_dilated_block(r, tm // r, LANES, tiles_per_seq) for _, r in DILATED_GROUPS]
    return pl.pallas_call(
        functools.partial(_merge_out_kernel, tm=tm),
        grid=(t // tm,),
        in_specs=[row(D_MODEL)] + o_specs + l_specs
                 + [row(ATTN_WIDTH), _resident(expand.shape),
                    _resident((ATTN_WIDTH, D_MODEL), layer)],
        out_specs=row(D_MODEL),
        out_shape=jax.ShapeDtypeStruct((t, D_MODEL), F32),
        scratch_shapes=[pltpu.VMEM((N_GROUPS - 1, tm, LANES), F32),
                        pltpu.VMEM((N_GROUPS - 1, LANE_TILES, tm, LANES), F32),
                        pltpu.VMEM((FAST_STRIDE, tm // FAST_STRIDE, LANES), F32),
                        pltpu.VMEM((tm, ATTN_WIDTH), BF16)],
        compiler_params=_params(),
        name="merge_out",
    )(x2, *outs, *stats, gate, expand, w_out)


def _attn_layer(x2, norm_g, w_in, q_gain, k_gain, w_out, rel_bias, layer, batch, seq):
    *qkvs, gate = _attn_proj(x2, norm_g, w_in, q_gain, k_gain, layer, batch, seq)
    outs, stats = [], []
    for g in range(N_GROUPS):
        o, st = _attn_group(qkvs[g], rel_bias, g)
        outs.append(o)
        stats.append(st)
    return _merge_out(x2, outs, stats, gate, w_out, layer, seq)


def kernel(x, conv_norm, conv_w_in, conv_w, conv_w_out, attn_norm, attn_w_in,
           attn_q_gain, attn_k_gain, attn_w_out, rel_bias):
    batch, seq, d = x.shape
    assert d == D_MODEL and seq % TOKEN_TILE == 0
    assert all(TOKEN_TILE % (8 * r) == 0 and seq % (BLOCK * r) == 0
               for _, r in DILATED_GROUPS)
    depth = conv_norm.shape[0] + attn_norm.shape[0]
    conv_p = (conv_norm[:, None, :], conv_w_in.astype(BF16), conv_w,
              conv_w_out.astype(BF16))
    attn_p = (attn_norm[:, None, :], attn_w_in.astype(BF16), attn_q_gain,
              attn_k_gain, attn_w_out.astype(BF16))
    x2 = x.reshape(batch * seq, d)
    for i in range(depth):
        j = i // 2
        if i % 2 == 0:
            x2 = _conv_layer(x2, *conv_p, j, seq)
        else:
            x2 = _attn_layer(x2, *attn_p, rel_bias, j, batch, seq)
    return x2.reshape(batch, seq, d)
```

```python
import functools
import math

import jax
import jax.numpy as jnp
from jax import lax
from jax.experimental import pallas as pl
from jax.experimental.pallas import tpu as pltpu

D_MODEL = 1024
CONV_WIDTH = 2048
CONV_K = 3
HEAD_DIM = 64
N_HEADS = 16
ATTN_WIDTH = N_HEADS * HEAD_DIM
DILATED_GROUPS = ((128, 1), (512, 4), (2048, 16))
N_GROUPS = len(DILATED_GROUPS)
QKV_COLS = N_GROUPS * 3 * ATTN_WIDTH
BLOCK = 128
N_BUCKETS = 32
MAX_DISTANCE = 2048
EPS = 1e-6
MASKED = -1e30

LANES = 128
MXU_COLS = 256
HEADS_PER_TILE = LANES // HEAD_DIM
N_PAIRS = N_HEADS // HEADS_PER_TILE
LANE_TILES = ATTN_WIDTH // LANES
VMEM_LIMIT = 56 * 1024 * 1024
TOKEN_TILE = 512
ATTN_BLOCKS_PER_STEP = 16
FAST_STRIDE = 4
LOG2E = math.log2(math.e)

F32 = jnp.float32
BF16 = jnp.bfloat16


def _resident(shape, layer=None):
    if layer is None:
        return pl.BlockSpec(shape, lambda *_: (0,) * len(shape),
                            pipeline_mode=pl.Buffered(1))
    return pl.BlockSpec((None,) + tuple(shape), lambda *_: (layer,) + (0,) * len(shape),
                        pipeline_mode=pl.Buffered(1))


def _cast_rider_specs(weights, layer, n_steps):
    in_specs, out_specs, out_shapes = [], [], []
    for w in weights:
        _, rows, cols = w.shape
        slab = rows // n_steps
        assert slab * n_steps == rows and slab % 16 == 0
        in_specs.append(pl.BlockSpec((None, slab, cols), lambda i: (layer, i, 0)))
        out_specs.append(pl.BlockSpec((slab, cols), lambda i: (i, 0)))
        out_shapes.append(jax.ShapeDtypeStruct((rows, cols), BF16))
    return in_specs, out_specs, out_shapes


def _params(n_axes=1):
    return pltpu.CompilerParams(
        dimension_semantics=("arbitrary",) * n_axes, vmem_limit_bytes=VMEM_LIMIT)


def _rms_rows(x, gain):
    ms = jnp.mean(x * x, axis=-1, keepdims=True)
    return x * lax.rsqrt(ms + EPS) * gain


def _conv_layer_kernel(*refs, tm, te, tiles_per_seq, n_riders):
    x_ref, g_ref, win_ref, wc_ref, wout_ref = refs[:5]
    rider_in = refs[5:5 + n_riders]
    o_ref = refs[5 + n_riders]
    rider_out = refs[6 + n_riders:6 + 2 * n_riders]
    h_ref, carry_ref = refs[6 + 2 * n_riders:]
    for src, dst in zip(rider_in, rider_out):
        dst[...] = src[...].astype(BF16)
    i = pl.program_id(0)
    x = x_ref[...]
    h_ref[...] = _rms_rows(x, g_ref[...]).astype(BF16)

    @pl.when(i % tiles_per_seq == 0)
    def _():
        carry_ref[...] = jnp.zeros_like(carry_ref)

    rows = lax.broadcasted_iota(jnp.int32, (tm, te), 0)
    for j in range(CONV_WIDTH // te):
        cols = slice(j * te, (j + 1) * te)

        def proj(p):
            w = win_ref[:, p * CONV_WIDTH + j * te:p * CONV_WIDTH + (j + 1) * te]
            return jnp.dot(h_ref[...], w, preferred_element_type=F32)

        v = proj(1) * proj(2)
        prev1 = carry_ref[1:2, cols]
        prev2 = carry_ref[0:1, cols]
        v1 = jnp.where(rows == 0, prev1, pltpu.roll(v, 1, 0))
        v2 = jnp.where(rows == 0, prev2,
                       jnp.where(rows == 1, prev1, pltpu.roll(v, 2, 0)))
        carry_ref[0:2, cols] = v[tm - 2:tm]
        wc = wc_ref[:, cols]
        conv = wc[0:1] * v2 + wc[1:2] * v1 + wc[2:3] * v
        y = proj(0) * conv * jax.nn.silu(proj(3))
        part = jnp.dot(y.astype(BF16), wout_ref[cols, :],
                       preferred_element_type=F32)
        if j == 0:
            o_ref[...] = x + part
        else:
            o_ref[...] += part


def _conv_layer(x2, norm_g, w_in, w_conv, w_out, layer, seq, next_weights):
    t = x2.shape[0]
    tm, te = 2 * TOKEN_TILE, MXU_COLS
    r_in, r_out, r_shapes = _cast_rider_specs(next_weights, layer, t // tm)
    kern = functools.partial(_conv_layer_kernel, tm=tm, te=te,
                             tiles_per_seq=seq // tm, n_riders=len(next_weights))
    out, *cast = pl.pallas_call(
        kern,
        grid=(t // tm,),
        in_specs=[
            pl.BlockSpec((tm, D_MODEL), lambda i: (i, 0)),
            _resident((1, D_MODEL), layer),
            _resident((D_MODEL, 4 * CONV_WIDTH)),
            _resident((CONV_K, CONV_WIDTH), layer),
            _resident((CONV_WIDTH, D_MODEL)),
        ] + r_in,
        out_specs=[pl.BlockSpec((tm, D_MODEL), lambda i: (i, 0))] + r_out,
        out_shape=[jax.ShapeDtypeStruct((t, D_MODEL), F32)] + r_shapes,
        scratch_shapes=[pltpu.VMEM((tm, D_MODEL), BF16),
                        pltpu.VMEM((8, CONV_WIDTH), F32)],
        compiler_params=_params(),
        name="conv_layer",
    )(x2, norm_g, w_in, w_conv, w_out, *next_weights)
    return out, cast


def _attn_proj_kernel(*refs, tm, n_riders):
    x_ref, g_ref, w_ref, qg_ref, kg_ref = refs[:5]
    rider_in = refs[5:5 + n_riders]
    qkv0_ref, qkv1_ref, qkv2_ref, gate_ref = refs[5 + n_riders:9 + n_riders]
    rider_out = refs[9 + n_riders:9 + 2 * n_riders]
    h_ref, slab_ref = refs[9 + 2 * n_riders:]
    for src, dst in zip(rider_in, rider_out):
        dst[...] = src[...].astype(BF16)
    h32 = _rms_rows(x_ref[...], g_ref[...])
    h_ref[0] = h32.astype(BF16)
    for j in range(LANE_TILES):
        slab_ref[j] = h32[:, j * LANES:(j + 1) * LANES]
    for g in range(1, N_GROUPS):
        r = DILATED_GROUPS[g][1]
        sub = tm // r
        for c in range(r):
            for j in range(LANE_TILES):
                piece = slab_ref[j, pl.ds(c, sub, stride=r), :]
                h_ref[g, c * sub:(c + 1) * sub, j * LANES:(j + 1) * LANES] = piece.astype(BF16)

    out_refs = (qkv0_ref, qkv1_ref, qkv2_ref)
    low = lax.broadcasted_iota(jnp.int32, (tm, LANES), 1) < HEAD_DIM
    for g in range(N_GROUPS):
        r = DILATED_GROUPS[g][1]
        sub = tm // r
        for part in range(3):
            col0 = (g * 3 + part) * ATTN_WIDTH
            a = jnp.dot(h_ref[g], w_ref[:, col0:col0 + ATTN_WIDTH],
                        preferred_element_type=F32)
            gain_ref = (qg_ref, kg_ref, None)[part]
            for s in range(LANE_TILES):
                sl = slice(s * LANES, (s + 1) * LANES)
                a_s = a[:, sl]
                if gain_ref is not None:
                    sq = a_s * a_s
                    ss_low = jnp.sum(jnp.where(low, sq, 0.0), axis=-1, keepdims=True)
                    ss_high = jnp.sum(jnp.where(low, 0.0, sq), axis=-1, keepdims=True)
                    ss = jnp.where(low, ss_low, ss_high)
                    a_s = a_s * lax.rsqrt(ss * (1.0 / HEAD_DIM) + EPS) * gain_ref[g:g + 1, sl]
                a_s = a_s.astype(BF16)
                for c in range(r):
                    out_refs[g][c, :, part * ATTN_WIDTH + s * LANES:
                                part * ATTN_WIDTH + (s + 1) * LANES] = a_s[c * sub:(c + 1) * sub]

    z = jnp.dot(h_ref[0], w_ref[:, QKV_COLS:QKV_COLS + ATTN_WIDTH],
                preferred_element_type=F32)
    gate_ref[...] = jax.nn.silu(z).astype(BF16)


def _dilated_block(r, sub, width, tiles_per_seq):
    return pl.BlockSpec((None, r, sub, width),
                        lambda i: (i // tiles_per_seq, 0, i % tiles_per_seq, 0))


def _attn_proj(x2, norm_g, w_in, q_gain, k_gain, layer, batch, seq, next_weights):
    t = x2.shape[0]
    tm = TOKEN_TILE
    tiles_per_seq = seq // tm
    qg = jnp.tile(q_gain, (1, 1, N_HEADS)) * (HEAD_DIM ** -0.5 * LOG2E)
    kg = jnp.tile(k_gain, (1, 1, N_HEADS))
    qkv_specs, qkv_shapes = [], []
    for _, r in DILATED_GROUPS:
        qkv_specs.append(_dilated_block(r, tm // r, 3 * ATTN_WIDTH, tiles_per_seq))
        qkv_shapes.append(jax.ShapeDtypeStruct((batch, r, seq // r, 3 * ATTN_WIDTH), BF16))
    r_in, r_out, r_shapes = _cast_rider_specs(next_weights, layer + 1, t // tm)
    return pl.pallas_call(
        functools.partial(_attn_proj_kernel, tm=tm, n_riders=len(next_weights)),
        grid=(t // tm,),
        in_specs=[
            pl.BlockSpec((tm, D_MODEL), lambda i: (i, 0)),
            _resident((1, D_MODEL), layer),
            _resident((D_MODEL, QKV_COLS + ATTN_WIDTH)),
            _resident((N_GROUPS, ATTN_WIDTH), layer),
            _resident((N_GROUPS, ATTN_WIDTH), layer),
        ] + r_in,
        out_specs=qkv_specs + [pl.BlockSpec((tm, ATTN_WIDTH), lambda i: (i, 0))] + r_out,
        out_shape=qkv_shapes + [jax.ShapeDtypeStruct((t, ATTN_WIDTH), BF16)] + r_shapes,
        scratch_shapes=[pltpu.VMEM((N_GROUPS, tm, D_MODEL), BF16),
                        pltpu.VMEM((LANE_TILES, tm, LANES), F32)],
        compiler_params=_params(),
        name="attn_proj",
    )(x2, norm_g, w_in, qg, kg, *next_weights)


def _attn_group_kernel(q_ref, k_ref, v_ref, bias_ref, o_ref, stat_ref, *scratch,
                       ncls, nq, carry):
    if carry:
        kprev_ref, vprev_ref = scratch
        n = pl.program_id(2)

        @pl.when(n == 0)
        def _():
            kprev_ref[...] = jnp.zeros_like(kprev_ref)
            vprev_ref[...] = jnp.zeros_like(vprev_ref)

        first = jnp.where(n == 0, 1, 0)
    else:
        first = 1
    lane = lax.broadcasted_iota(jnp.int32, (BLOCK, LANES), 1)
    low = lane < HEAD_DIM
    lane_row = lax.broadcasted_iota(jnp.int32, (1, LANES), 1)
    low_b = (lane_row < HEAD_DIM).astype(F32).astype(BF16)
    high_b = (lane_row >= HEAD_DIM).astype(F32).astype(BF16)
    ones = jnp.ones((2 * BLOCK, LANES), BF16)
    no_prev = jnp.zeros((BLOCK, LANES), BF16)

    for ci in range(ncls):
        for qb in range(nq):
            rows = slice(qb * BLOCK, (qb + 1) * BLOCK)
            stat_tile = jnp.zeros((BLOCK, LANES), F32)
            for pr in range(N_PAIRS):
                cols = slice(pr * LANES, (pr + 1) * LANES)
                q = q_ref[ci, rows, cols]
                q2 = jnp.concatenate([q * low_b, q * high_b], axis=0)
                if qb == 0:
                    kp = kprev_ref[:, cols] if carry else no_prev
                    vp = vprev_ref[:, cols] if carry else no_prev
                    kcat = jnp.concatenate([kp, k_ref[ci, 0:BLOCK, cols]], axis=0)
                    vcat = jnp.concatenate([vp, v_ref[ci, 0:BLOCK, cols]], axis=0)
                    bidx = first
                else:
                    krows = slice((qb - 1) * BLOCK, (qb + 1) * BLOCK)
                    kcat = k_ref[ci, krows, cols]
                    vcat = v_ref[ci, krows, cols]
                    bidx = 0
                s = lax.dot_general(q2, kcat, (((1,), (1,)), ((), ())),
                                    preferred_element_type=F32) + bias_ref[bidx, pr]
                m = jnp.max(s, axis=1, keepdims=True)
                p = jnp.exp2(s - m).astype(BF16)
                oe = jnp.dot(p, jnp.concatenate([vcat, ones], axis=1),
                             preferred_element_type=F32)
                o = jnp.where(low, oe[0:BLOCK, :LANES], oe[BLOCK:2 * BLOCK, :LANES])
                o_ref[ci, rows, cols] = o.astype(BF16)
                l = oe[:, LANES:]
                for hh in range(HEADS_PER_TILE):
                    head = HEADS_PER_TILE * pr + hh
                    hrows = slice(hh * BLOCK, (hh + 1) * BLOCK)
                    stat_tile = jnp.where(lane == head, m[hrows], stat_tile)
                    stat_tile = jnp.where(lane == N_HEADS + head, l[hrows], stat_tile)
            stat_ref[ci, rows, :] = stat_tile

    if carry:
        kprev_ref[...] = k_ref[0, (nq - 1) * BLOCK:nq * BLOCK, :]
        vprev_ref[...] = v_ref[0, (nq - 1) * BLOCK:nq * BLOCK, :]


def _group_bias(rel_bias, g, dilation, span):
    a = jnp.arange(BLOCK)[:, None]
    kk = jnp.arange(2 * BLOCK)[None, :]
    step = BLOCK + a - kk
    valid = (step >= 0) & (step <= span)
    dist = jnp.clip(step, 0, span) * dilation
    max_exact = N_BUCKETS // 2
    d = jnp.maximum(dist, 1).astype(F32)
    large = max_exact + (jnp.log(d / max_exact) / math.log(MAX_DISTANCE / max_exact)
                         * (N_BUCKETS - max_exact)).astype(jnp.int32)
    large = jnp.minimum(large, N_BUCKETS - 1)
    bucket = jnp.where(dist < max_exact, dist, large)
    tab = rel_bias[:, g * N_HEADS:(g + 1) * N_HEADS].astype(F32)
    onehot = (bucket[..., None] == jnp.arange(N_BUCKETS)).astype(F32)
    bias = jnp.einsum("akn,nh->hak", onehot, tab, precision=lax.Precision.HIGHEST) * LOG2E
    interior = jnp.where(valid[None], bias, MASKED)
    first = jnp.where((valid & (kk >= BLOCK))[None], bias, MASKED)
    both = jnp.stack([interior, first])
    return both.reshape(2, N_PAIRS, HEADS_PER_TILE * BLOCK, 2 * BLOCK)


def _attn_group(qkv, rel_bias, g):
    batch, dilation, sub_len, _ = qkv.shape
    span = DILATED_GROUPS[g][0] // dilation
    nq = min(ATTN_BLOCKS_PER_STEP, sub_len // BLOCK)
    rows = nq * BLOCK
    carry = rows < sub_len
    ncls = 1 if carry else min(dilation, ATTN_BLOCKS_PER_STEP // nq)
    bias = _group_bias(rel_bias, g, dilation, span)

    def spec(width, col):
        return pl.BlockSpec((None, ncls, rows, width), lambda b, c, n: (b, c, n, col))

    scratch = [pltpu.VMEM((BLOCK, ATTN_WIDTH), BF16)] * 2 if carry else []
    return pl.pallas_call(
        functools.partial(_attn_group_kernel, ncls=ncls, nq=nq, carry=carry),
        grid=(batch, dilation // ncls, sub_len // rows),
        in_specs=[spec(ATTN_WIDTH, 0), spec(ATTN_WIDTH, 1), spec(ATTN_WIDTH, 2),
                  _resident(bias.shape)],
        out_specs=[spec(ATTN_WIDTH, 0), spec(LANES, 0)],
        out_shape=[jax.ShapeDtypeStruct((batch, dilation, sub_len, ATTN_WIDTH), BF16),
                   jax.ShapeDtypeStruct((batch, dilation, sub_len, LANES), F32)],
        scratch_shapes=scratch,
        compiler_params=_params(3),
        name=f"attn_group{g}",
    )(qkv, qkv, qkv, bias)


def _merge_out_kernel(x_ref, o0_ref, o1_ref, o2_ref, s0_ref, s1_ref, s2_ref,
                      gate_ref, expand_ref, w_ref, out_ref,
                      snat_ref, onat_ref, stage_ref, y_ref, *, tm):
    o_refs = (o0_ref, o1_ref, o2_ref)
    s_refs = (s0_ref, s1_ref, s2_ref)

    def to_natural(dst, piece, r):
        sub = tm // r
        if r <= FAST_STRIDE:
            for c in range(r):
                dst[pl.ds(c, sub, stride=r), :] = piece(c)
            return
        outer = r // FAST_STRIDE
        for c0 in range(FAST_STRIDE):
            for c1 in range(outer):
                stage_ref[c0, pl.ds(c1, sub, stride=outer), :] = piece(c1 * FAST_STRIDE + c0)
            dst[pl.ds(c0, sub * outer, stride=FAST_STRIDE), :] = stage_ref[c0]

    assert DILATED_GROUPS[0][1] == 1
    for g in range(1, N_GROUPS):
        r = DILATED_GROUPS[g][1]
        to_natural(snat_ref.at[g - 1], lambda c: s_refs[g][c], r)
        for j in range(LANE_TILES):
            to_natural(onat_ref.at[g - 1, j],
                       lambda c: o_refs[g][c, :, j * LANES:(j + 1) * LANES].astype(F32), r)

    def o_tile(g, j):
        if g == 0:
            return o0_ref[0, :, j * LANES:(j + 1) * LANES].astype(F32)
        return onat_ref[g - 1, j]

    stats = [s0_ref[0]] + [snat_ref[g - 1] for g in range(1, N_GROUPS)]
    m = jnp.maximum(jnp.maximum(stats[0], stats[1]), stats[2])
    es = [jnp.exp2(st - m) for st in stats]
    den = sum(e * pltpu.roll(st, LANES - N_HEADS, 1) for e, st in zip(es, stats))
    head_lane = lax.broadcasted_iota(jnp.int32, (tm, LANES), 1) < N_HEADS
    inv = 1.0 / jnp.where(head_lane, den, 1.0)
    spreads = []
    for e in es:
        alpha = jnp.where(head_lane, e * inv, 0.0)
        hi = alpha.astype(BF16)
        lo = (alpha - hi.astype(F32)).astype(BF16)
        spreads.append(jnp.dot(jnp.concatenate([hi, lo], axis=1), expand_ref[...],
                               preferred_element_type=F32))
    for j in range(LANE_TILES):
        cols = slice(j * LANES, (j + 1) * LANES)
        o = sum(spreads[g][:, cols] * o_tile(g, j) for g in range(N_GROUPS))
        y_ref[:, cols] = (o * gate_ref[:, cols].astype(F32)).astype(BF16)
    out_ref[...] = x_ref[...] + jnp.dot(y_ref[...], w_ref[...],
                                        preferred_element_type=F32)


def _merge_out(x2, outs, stats, gate, w_out, seq):
    t = x2.shape[0]
    tm = TOKEN_TILE
    tiles_per_seq = seq // tm
    lane_head = jnp.arange(2 * LANES) % LANES
    expand = (lane_head[:, None] == (jnp.arange(ATTN_WIDTH) // HEAD_DIM)[None, :]).astype(BF16)
    row = lambda w: pl.BlockSpec((tm, w), lambda i: (i, 0))
    o_specs = [_dilated_block(r, tm // r, ATTN_WIDTH, tiles_per_seq) for _, r in DILATED_GROUPS]
    l_specs = [_dilated_block(r, tm // r, LANES, tiles_per_seq) for _, r in DILATED_GROUPS]
    return pl.pallas_call(
        functools.partial(_merge_out_kernel, tm=tm),
        grid=(t // tm,),
        in_specs=[row(D_MODEL)] + o_specs + l_specs
                 + [row(ATTN_WIDTH), _resident(expand.shape),
                    _resident((ATTN_WIDTH, D_MODEL))],
        out_specs=row(D_MODEL),
        out_shape=jax.ShapeDtypeStruct((t, D_MODEL), F32),
        scratch_shapes=[pltpu.VMEM((N_GROUPS - 1, tm, LANES), F32),
                        pltpu.VMEM((N_GROUPS - 1, LANE_TILES, tm, LANES), F32),
                        pltpu.VMEM((FAST_STRIDE, tm // FAST_STRIDE, LANES), F32),
                        pltpu.VMEM((tm, ATTN_WIDTH), BF16)],
        compiler_params=_params(),
        name="merge_out",
    )(x2, *outs, *stats, gate, expand, w_out)


def _attn_layer(x2, norm_g, w_in, q_gain, k_gain, w_out, rel_bias, layer, batch, seq,
                next_weights):
    qkv0, qkv1, qkv2, gate, *cast = _attn_proj(x2, norm_g, w_in, q_gain, k_gain, layer,
                                               batch, seq, next_weights)
    outs, stats = [], []
    for g, qkv in enumerate((qkv0, qkv1, qkv2)):
        o, st = _attn_group(qkv, rel_bias, g)
        outs.append(o)
        stats.append(st)
    return _merge_out(x2, outs, stats, gate, w_out, seq), cast


def kernel(x, conv_norm, conv_w_in, conv_w, conv_w_out, attn_norm, attn_w_in,
           attn_q_gain, attn_k_gain, attn_w_out, rel_bias):
    batch, seq, d = x.shape
    assert d == D_MODEL and seq % TOKEN_TILE == 0
    assert all(TOKEN_TILE % (8 * r) == 0 and seq % (BLOCK * r) == 0
               for _, r in DILATED_GROUPS)
    depth = conv_norm.shape[0] + attn_norm.shape[0]
    conv_norm3, attn_norm3 = conv_norm[:, None, :], attn_norm[:, None, :]
    w_in, w_out = conv_w_in[0].astype(BF16), conv_w_out[0].astype(BF16)
    x2 = x.reshape(batch * seq, d)
    for i in range(depth):
        j = i // 2
        if i % 2 == 0:
            nxt = (attn_w_in, attn_w_out) if i + 1 < depth else ()
            x2, cast = _conv_layer(x2, conv_norm3, w_in, conv_w, w_out, j, seq, nxt)
        else:
            nxt = (conv_w_in, conv_w_out) if i + 1 < depth else ()
            x2, cast = _attn_layer(x2, attn_norm3, w_in, attn_q_gain, attn_k_gain, w_out,
                                   rel_bias, j, batch, seq, nxt)
        if cast:
            w_in, w_out = cast
    return x2.reshape(batch, seq, d)
```

```python
import functools
import math

import jax
import jax.numpy as jnp
from jax import lax
from jax.experimental import pallas as pl
from jax.experimental.pallas import tpu as pltpu

D_MODEL = 1024
CONV_WIDTH = 2048
CONV_K = 3
HEAD_DIM = 64
N_HEADS = 16
ATTN_WIDTH = N_HEADS * HEAD_DIM
DILATED_GROUPS = ((128, 1), (512, 4), (2048, 16))
N_GROUPS = len(DILATED_GROUPS)
QKV_COLS = N_GROUPS * 3 * ATTN_WIDTH
BLOCK = 128
N_BUCKETS = 32
MAX_DISTANCE = 2048
EPS = 1e-6
MASKED = -1e30

LANES = 128
MXU_COLS = 256
HEADS_PER_TILE = LANES // HEAD_DIM
N_PAIRS = N_HEADS // HEADS_PER_TILE
LANE_TILES = ATTN_WIDTH // LANES
VMEM_LIMIT = 56 * 1024 * 1024
TOKEN_TILE = 512
ATTN_BLOCKS_PER_STEP = 16
FAST_STRIDE = 4
LOG2E = math.log2(math.e)

F32 = jnp.float32
BF16 = jnp.bfloat16


def _resident(shape, layer=None):
    if layer is None:
        return pl.BlockSpec(shape, lambda *_: (0,) * len(shape),
                            pipeline_mode=pl.Buffered(1))
    return pl.BlockSpec((None,) + tuple(shape), lambda *_: (layer,) + (0,) * len(shape),
                        pipeline_mode=pl.Buffered(1))


def _cast_rider_specs(weights, layer, n_steps):
    in_specs, out_specs, out_shapes = [], [], []
    for w in weights:
        _, rows, cols = w.shape
        slab = rows // n_steps
        assert slab * n_steps == rows and slab % 16 == 0
        in_specs.append(pl.BlockSpec((None, slab, cols), lambda i: (layer, i, 0)))
        out_specs.append(pl.BlockSpec((slab, cols), lambda i: (i, 0)))
        out_shapes.append(jax.ShapeDtypeStruct((rows, cols), BF16))
    return in_specs, out_specs, out_shapes


def _params(n_axes=1):
    return pltpu.CompilerParams(
        dimension_semantics=("arbitrary",) * n_axes, vmem_limit_bytes=VMEM_LIMIT)


def _rms_rows(x, gain):
    ms = jnp.mean(x * x, axis=-1, keepdims=True)
    return x * lax.rsqrt(ms + EPS) * gain


def _conv_layer_kernel(*refs, tm, te, tiles_per_seq, n_riders):
    x_ref, g_ref, win_ref, wc_ref, wout_ref = refs[:5]
    rider_in = refs[5:5 + n_riders]
    o_ref = refs[5 + n_riders]
    rider_out = refs[6 + n_riders:6 + 2 * n_riders]
    h_ref, carry_ref = refs[6 + 2 * n_riders:]
    for src, dst in zip(rider_in, rider_out):
        dst[...] = src[...].astype(BF16)
    i = pl.program_id(0)
    x = x_ref[...]
    h_ref[...] = _rms_rows(x, g_ref[...]).astype(BF16)

    @pl.when(i % tiles_per_seq == 0)
    def _():
        carry_ref[...] = jnp.zeros_like(carry_ref)

    rows = lax.broadcasted_iota(jnp.int32, (tm, te), 0)
    for j in range(CONV_WIDTH // te):
        cols = slice(j * te, (j + 1) * te)

        def proj(p):
            w = win_ref[:, p * CONV_WIDTH + j * te:p * CONV_WIDTH + (j + 1) * te]
            return jnp.dot(h_ref[...], w, preferred_element_type=F32)

        v = proj(1) * proj(2)
        prev1 = carry_ref[1:2, cols]
        prev2 = carry_ref[0:1, cols]
        v1 = jnp.where(rows == 0, prev1, pltpu.roll(v, 1, 0))
        v2 = jnp.where(rows == 0, prev2,
                       jnp.where(rows == 1, prev1, pltpu.roll(v, 2, 0)))
        carry_ref[0:2, cols] = v[tm - 2:tm]
        wc = wc_ref[:, cols]
        conv = wc[0:1] * v2 + wc[1:2] * v1 + wc[2:3] * v
        y = proj(0) * conv * jax.nn.silu(proj(3))
        part = jnp.dot(y.astype(BF16), wout_ref[cols, :],
                       preferred_element_type=F32)
        if j == 0:
            o_ref[...] = x + part
        else:
            o_ref[...] += part


def _conv_layer(x2, norm_g, w_in, w_conv, w_out, layer, seq, next_weights):
    t = x2.shape[0]
    tm, te = 2 * TOKEN_TILE, MXU_COLS
    r_in, r_out, r_shapes = _cast_rider_specs(next_weights, layer, t // tm)
    kern = functools.partial(_conv_layer_kernel, tm=tm, te=te,
                             tiles_per_seq=seq // tm, n_riders=len(next_weights))
    out, *cast = pl.pallas_call(
        kern,
        grid=(t // tm,),
        in_specs=[
            pl.BlockSpec((tm, D_MODEL), lambda i: (i, 0)),
            _resident((1, D_MODEL), layer),
            _resident((D_MODEL, 4 * CONV_WIDTH)),
            _resident((CONV_K, CONV_WIDTH), layer),
            _resident((CONV_WIDTH, D_MODEL)),
        ] + r_in,
        out_specs=[pl.BlockSpec((tm, D_MODEL), lambda i: (i, 0))] + r_out,
        out_shape=[jax.ShapeDtypeStruct((t, D_MODEL), F32)] + r_shapes,
        scratch_shapes=[pltpu.VMEM((tm, D_MODEL), BF16),
                        pltpu.VMEM((8, CONV_WIDTH), F32)],
        compiler_params=_params(),
        name="conv_layer",
    )(x2, norm_g, w_in, w_conv, w_out, *next_weights)
    return out, cast


def _attn_proj_kernel(*refs, tm, n_riders):
    x_ref, g_ref, w_ref, qg_ref, kg_ref = refs[:5]
    rider_in = refs[5:5 + n_riders]
    qkv0_ref, qkv1_ref, qkv2_ref, gate_ref = refs[5 + n_riders:9 + n_riders]
    rider_out = refs[9 + n_riders:9 + 2 * n_riders]
    h_ref, slab_ref = refs[9 + 2 * n_riders:]
    for src, dst in zip(rider_in, rider_out):
        dst[...] = src[...].astype(BF16)
    h32 = _rms_rows(x_ref[...], g_ref[...])
    h_ref[0] = h32.astype(BF16)
    for j in range(LANE_TILES):
        slab_ref[j] = h32[:, j * LANES:(j + 1) * LANES]
    for g in range(1, N_GROUPS):
        r = DILATED_GROUPS[g][1]
        sub = tm // r
        for c in range(r):
            for j in range(LANE_TILES):
                piece = slab_ref[j, pl.ds(c, sub, stride=r), :]
                h_ref[g, c * sub:(c + 1) * sub, j * LANES:(j + 1) * LANES] = piece.astype(BF16)

    z = jnp.dot(h_ref[0], w_ref[:, QKV_COLS:QKV_COLS + ATTN_WIDTH],
                preferred_element_type=F32)
    gate_ref[...] = jax.nn.silu(z).astype(BF16)

    out_refs = (qkv0_ref, qkv1_ref, qkv2_ref)
    low = lax.broadcasted_iota(jnp.int32, (tm, LANES), 1) < HEAD_DIM
    for g in range(N_GROUPS):
        r = DILATED_GROUPS[g][1]
        sub = tm // r
        for part in range(3):
            col0 = (g * 3 + part) * ATTN_WIDTH
            a = jnp.dot(h_ref[g], w_ref[:, col0:col0 + ATTN_WIDTH],
                        preferred_element_type=F32)
            gain_ref = (qg_ref, kg_ref, None)[part]
            for s in range(LANE_TILES):
                sl = slice(s * LANES, (s + 1) * LANES)
                a_s = a[:, sl]
                if gain_ref is not None:
                    sq = a_s * a_s
                    ss_low = jnp.sum(jnp.where(low, sq, 0.0), axis=-1, keepdims=True)
                    ss_high = jnp.sum(jnp.where(low, 0.0, sq), axis=-1, keepdims=True)
                    ss = jnp.where(low, ss_low, ss_high)
                    a_s = a_s * lax.rsqrt(ss * (1.0 / HEAD_DIM) + EPS) * gain_ref[g:g + 1, sl]
                a_s = a_s.astype(BF16)
                for c in range(r):
                    out_refs[g][c, :, part * ATTN_WIDTH + s * LANES:
                                part * ATTN_WIDTH + (s + 1) * LANES] = a_s[c * sub:(c + 1) * sub]


def _dilated_block(r, sub, width, tiles_per_seq):
    return pl.BlockSpec((None, r, sub, width),
                        lambda i: (i // tiles_per_seq, 0, i % tiles_per_seq, 0))


def _attn_proj(x2, norm_g, w_in, q_gain, k_gain, layer, batch, seq, next_weights):
    t = x2.shape[0]
    tm = TOKEN_TILE
    tiles_per_seq = seq // tm
    qg = jnp.tile(q_gain, (1, 1, N_HEADS)) * (HEAD_DIM ** -0.5 * LOG2E)
    kg = jnp.tile(k_gain, (1, 1, N_HEADS))
    qkv_specs, qkv_shapes = [], []
    for _, r in DILATED_GROUPS:
        qkv_specs.append(_dilated_block(r, tm // r, 3 * ATTN_WIDTH, tiles_per_seq))
        qkv_shapes.append(jax.ShapeDtypeStruct((batch, r, seq // r, 3 * ATTN_WIDTH), BF16))
    r_in, r_out, r_shapes = _cast_rider_specs(next_weights, layer + 1, t // tm)
    return pl.pallas_call(
        functools.partial(_attn_proj_kernel, tm=tm, n_riders=len(next_weights)),
        grid=(t // tm,),
        in_specs=[
            pl.BlockSpec((tm, D_MODEL), lambda i: (i, 0)),
            _resident((1, D_MODEL), layer),
            _resident((D_MODEL, QKV_COLS + ATTN_WIDTH)),
            _resident((N_GROUPS, ATTN_WIDTH), layer),
            _resident((N_GROUPS, ATTN_WIDTH), layer),
        ] + r_in,
        out_specs=qkv_specs + [pl.BlockSpec((tm, ATTN_WIDTH), lambda i: (i, 0))] + r_out,
        out_shape=qkv_shapes + [jax.ShapeDtypeStruct((t, ATTN_WIDTH), BF16)] + r_shapes,
        scratch_shapes=[pltpu.VMEM((N_GROUPS, tm, D_MODEL), BF16),
                        pltpu.VMEM((LANE_TILES, tm, LANES), F32)],
        compiler_params=_params(),
        name="attn_proj",
    )(x2, norm_g, w_in, qg, kg, *next_weights)


def _attn_group_kernel(q_ref, k_ref, v_ref, bias_ref, o_ref, stat_ref, *scratch,
                       ncls, nq, carry):
    if carry:
        kprev_ref, vprev_ref = scratch
        n = pl.program_id(2)

        @pl.when(n == 0)
        def _():
            kprev_ref[...] = jnp.zeros_like(kprev_ref)
            vprev_ref[...] = jnp.zeros_like(vprev_ref)

        first = jnp.where(n == 0, 1, 0)
    else:
        first = 1
    lane = lax.broadcasted_iota(jnp.int32, (BLOCK, LANES), 1)
    low = lane < HEAD_DIM
    lane_row = lax.broadcasted_iota(jnp.int32, (1, LANES), 1)
    low_b = (lane_row < HEAD_DIM).astype(F32).astype(BF16)
    high_b = (lane_row >= HEAD_DIM).astype(F32).astype(BF16)
    ones = jnp.ones((2 * BLOCK, LANES), BF16)
    no_prev = jnp.zeros((BLOCK, LANES), BF16)

    for ci in range(ncls):
        for qb in range(nq):
            rows = slice(qb * BLOCK, (qb + 1) * BLOCK)
            stat_tile = jnp.zeros((BLOCK, LANES), F32)
            for pr in range(N_PAIRS):
                cols = slice(pr * LANES, (pr + 1) * LANES)
                q = q_ref[ci, rows, cols]
                q2 = jnp.concatenate([q * low_b, q * high_b], axis=0)
                if qb == 0:
                    kp = kprev_ref[:, cols] if carry else no_prev
                    vp = vprev_ref[:, cols] if carry else no_prev
                    kcat = jnp.concatenate([kp, k_ref[ci, 0:BLOCK, cols]], axis=0)
                    vcat = jnp.concatenate([vp, v_ref[ci, 0:BLOCK, cols]], axis=0)
                    bidx = first
                else:
                    krows = slice((qb - 1) * BLOCK, (qb + 1) * BLOCK)
                    kcat = k_ref[ci, krows, cols]
                    vcat = v_ref[ci, krows, cols]
                    bidx = 0
                s = lax.dot_general(q2, kcat, (((1,), (1,)), ((), ())),
                                    preferred_element_type=F32) + bias_ref[bidx, pr]
                m = jnp.max(s, axis=1, keepdims=True)
                p = jnp.exp2(s - m).astype(BF16)
                oe = jnp.dot(p, jnp.concatenate([vcat, ones], axis=1),
                             preferred_element_type=F32)
                o = jnp.where(low, oe[0:BLOCK, :LANES], oe[BLOCK:2 * BLOCK, :LANES])
                o_ref[ci, rows, cols] = o.astype(BF16)
                l = oe[:, LANES:]
                for hh in range(HEADS_PER_TILE):
                    head = HEADS_PER_TILE * pr + hh
                    hrows = slice(hh * BLOCK, (hh + 1) * BLOCK)
                    stat_tile = jnp.where(lane == head, m[hrows], stat_tile)
                    stat_tile = jnp.where(lane == N_HEADS + head, l[hrows], stat_tile)
            stat_ref[ci, rows, :] = stat_tile

    if carry:
        kprev_ref[...] = k_ref[0, (nq - 1) * BLOCK:nq * BLOCK, :]
        vprev_ref[...] = v_ref[0, (nq - 1) * BLOCK:nq * BLOCK, :]


def _group_bias(rel_bias, g, dilation, span):
    a = jnp.arange(BLOCK)[:, None]
    kk = jnp.arange(2 * BLOCK)[None, :]
    step = BLOCK + a - kk
    valid = (step >= 0) & (step <= span)
    dist = jnp.clip(step, 0, span) * dilation
    max_exact = N_BUCKETS // 2
    d = jnp.maximum(dist, 1).astype(F32)
    large = max_exact + (jnp.log(d / max_exact) / math.log(MAX_DISTANCE / max_exact)
                         * (N_BUCKETS - max_exact)).astype(jnp.int32)
    large = jnp.minimum(large, N_BUCKETS - 1)
    bucket = jnp.where(dist < max_exact, dist, large)
    tab = rel_bias[:, g * N_HEADS:(g + 1) * N_HEADS].astype(F32)
    onehot = (bucket[..., None] == jnp.arange(N_BUCKETS)).astype(F32)
    bias = jnp.einsum("akn,nh->hak", onehot, tab, precision=lax.Precision.HIGHEST) * LOG2E
    interior = jnp.where(valid[None], bias, MASKED)
    first = jnp.where((valid & (kk >= BLOCK))[None], bias, MASKED)
    both = jnp.stack([interior, first])
    return both.reshape(2, N_PAIRS, HEADS_PER_TILE * BLOCK, 2 * BLOCK)


def _attn_group(qkv, rel_bias, g):
    batch, dilation, sub_len, _ = qkv.shape
    span = DILATED_GROUPS[g][0] // dilation
    nq = min(ATTN_BLOCKS_PER_STEP, sub_len // BLOCK)
    rows = nq * BLOCK
    carry = rows < sub_len
    ncls = 1 if carry else min(dilation, ATTN_BLOCKS_PER_STEP // nq)
    bias = _group_bias(rel_bias, g, dilation, span)

    def spec(width, col):
        return pl.BlockSpec((None, ncls, rows, width), lambda b, c, n: (b, c, n, col))

    scratch = [pltpu.VMEM((BLOCK, ATTN_WIDTH), BF16)] * 2 if carry else []
    return pl.pallas_call(
        functools.partial(_attn_group_kernel, ncls=ncls, nq=nq, carry=carry),
        grid=(batch, dilation // ncls, sub_len // rows),
        in_specs=[spec(ATTN_WIDTH, 0), spec(ATTN_WIDTH, 1), spec(ATTN_WIDTH, 2),
                  _resident(bias.shape)],
        out_specs=[spec(ATTN_WIDTH, 0), spec(LANES, 0)],
        out_shape=[jax.ShapeDtypeStruct((batch, dilation, sub_len, ATTN_WIDTH), BF16),
                   jax.ShapeDtypeStruct((batch, dilation, sub_len, LANES), F32)],
        scratch_shapes=scratch,
        compiler_params=_params(3),
        name=f"attn_group{g}",
    )(qkv, qkv, qkv, bias)


def _merge_out_kernel(x_ref, o0_ref, o1_ref, o2_ref, s0_ref, s1_ref, s2_ref,
                      gate_ref, expand_ref, w_ref, out_ref,
                      snat_ref, onat_ref, stage_ref, y_ref, *, tm):
    o_refs = (o0_ref, o1_ref, o2_ref)
    s_refs = (s0_ref, s1_ref, s2_ref)

    def to_natural(dst, piece, r):
        sub = tm // r
        if r <= FAST_STRIDE:
            for c in range(r):
                dst[pl.ds(c, sub, stride=r), :] = piece(c)
            return
        outer = r // FAST_STRIDE
        for c0 in range(FAST_STRIDE):
            for c1 in range(outer):
                stage_ref[c0, pl.ds(c1, sub, stride=outer), :] = piece(c1 * FAST_STRIDE + c0)
            dst[pl.ds(c0, sub * outer, stride=FAST_STRIDE), :] = stage_ref[c0]

    assert DILATED_GROUPS[0][1] == 1
    for g in range(1, N_GROUPS):
        r = DILATED_GROUPS[g][1]
        to_natural(snat_ref.at[g - 1], lambda c: s_refs[g][c], r)
        for j in range(LANE_TILES):
            to_natural(onat_ref.at[g - 1, j],
                       lambda c: o_refs[g][c, :, j * LANES:(j + 1) * LANES].astype(F32), r)

    def o_tile(g, j):
        if g == 0:
            return o0_ref[0, :, j * LANES:(j + 1) * LANES].astype(F32)
        return onat_ref[g - 1, j]

    stats = [s0_ref[0]] + [snat_ref[g - 1] for g in range(1, N_GROUPS)]
    m = jnp.maximum(jnp.maximum(stats[0], stats[1]), stats[2])
    es = [jnp.exp2(st - m) for st in stats]
    den = sum(e * pltpu.roll(st, LANES - N_HEADS, 1) for e, st in zip(es, stats))
    head_lane = lax.broadcasted_iota(jnp.int32, (tm, LANES), 1) < N_HEADS
    inv = 1.0 / jnp.where(head_lane, den, 1.0)
    spreads = []
    for e in es:
        alpha = jnp.where(head_lane, e * inv, 0.0)
        hi = alpha.astype(BF16)
        lo = (alpha - hi.astype(F32)).astype(BF16)
        spreads.append(jnp.dot(jnp.concatenate([hi, lo], axis=1), expand_ref[...],
                               preferred_element_type=F32))
    for j in range(LANE_TILES):
        cols = slice(j * LANES, (j + 1) * LANES)
        o = sum(spreads[g][:, cols] * o_tile(g, j) for g in range(N_GROUPS))
        y_ref[:, cols] = (o * gate_ref[:, cols].astype(F32)).astype(BF16)
    out_ref[...] = x_ref[...] + jnp.dot(y_ref[...], w_ref[...],
                                        preferred_element_type=F32)


def _merge_out(x2, outs, stats, gate, w_out, seq):
    t = x2.shape[0]
    tm = 2 * TOKEN_TILE
    tiles_per_seq = seq // tm
    lane_head = jnp.arange(2 * LANES) % LANES
    expand = (lane_head[:, None] == (jnp.arange(ATTN_WIDTH) // HEAD_DIM)[None, :]).astype(BF16)
    row = lambda w: pl.BlockSpec((tm, w), lambda i: (i, 0))
    o_specs = [_dilated_block(r, tm // r, ATTN_WIDTH, tiles_per_seq) for _, r in DILATED_GROUPS]
    l_specs = [_dilated_block(r, tm // r, LANES, tiles_per_seq) for _, r in DILATED_GROUPS]
    return pl.pallas_call(
        functools.partial(_merge_out_kernel, tm=tm),
        grid=(t // tm,),
        in_specs=[row(D_MODEL)] + o_specs + l_specs
                 + [row(ATTN_WIDTH), _resident(expand.shape),
                    _resident((ATTN_WIDTH, D_MODEL))],
        out_specs=row(D_MODEL),
        out_shape=jax.ShapeDtypeStruct((t, D_MODEL), F32),
        scratch_shapes=[pltpu.VMEM((N_GROUPS - 1, tm, LANES), F32),
                        pltpu.VMEM((N_GROUPS - 1, LANE_TILES, tm, LANES), F32),
                        pltpu.VMEM((FAST_STRIDE, tm // FAST_STRIDE, LANES), F32),
                        pltpu.VMEM((tm, ATTN_WIDTH), BF16)],
        compiler_params=_params(),
        name="merge_out",
    )(x2, *outs, *stats, gate, expand, w_out)


def _attn_layer(x2, norm_g, w_in, q_gain, k_gain, w_out, rel_bias, layer, batch, seq,
                next_weights):
    qkv0, qkv1, qkv2, gate, *cast = _attn_proj(x2, norm_g, w_in, q_gain, k_gain, layer,
                                               batch, seq, next_weights)
    outs, stats = [], []
    for g, qkv in enumerate((qkv0, qkv1, qkv2)):
        o, st = _attn_group(qkv, rel_bias, g)
        outs.append(o)
        stats.append(st)
    return _merge_out(x2, outs, stats, gate, w_out, seq), cast


def kernel(x, conv_norm, conv_w_in, conv_w, conv_w_out, attn_norm, attn_w_in,
           attn_q_gain, attn_k_gain, attn_w_out, rel_bias):
    batch, seq, d = x.shape
    assert d == D_MODEL and seq % TOKEN_TILE == 0
    assert all(TOKEN_TILE % (8 * r) == 0 and seq % (BLOCK * r) == 0
               for _, r in DILATED_GROUPS)
    depth = conv_norm.shape[0] + attn_norm.shape[0]
    conv_norm3, attn_norm3 = conv_norm[:, None, :], attn_norm[:, None, :]
    w_in, w_out = conv_w_in[0].astype(BF16), conv_w_out[0].astype(BF16)
    x2 = x.reshape(batch * seq, d)
    for i in range(depth):
        j = i // 2
        if i % 2 == 0:
            nxt = (attn_w_in, attn_w_out) if i + 1 < depth else ()
            x2, cast = _conv_layer(x2, conv_norm3, w_in, conv_w, w_out, j, seq, nxt)
        else:
            nxt = (conv_w_in, conv_w_out) if i + 1 < depth else ()
            x2, cast = _attn_layer(x2, attn_norm3, w_in, attn_q_gain, attn_k_gain, w_out,
                                   rel_bias, j, batch, seq, nxt)
        if cast:
            w_in, w_out = cast
    return x2.reshape(batch, seq, d)
```

```python
import functools
import math

import jax
import jax.numpy as jnp
from jax import lax
from jax.experimental import pallas as pl
from jax.experimental.pallas import tpu as pltpu

D_MODEL = 1024
CONV_WIDTH = 2048
CONV_K = 3
HEAD_DIM = 64
N_HEADS = 16
ATTN_WIDTH = N_HEADS * HEAD_DIM
DILATED_GROUPS = ((128, 1), (512, 4), (2048, 16))
N_GROUPS = len(DILATED_GROUPS)
QKV_COLS = N_GROUPS * 3 * ATTN_WIDTH
BLOCK = 128
N_BUCKETS = 32
MAX_DISTANCE = 2048
EPS = 1e-6
MASKED = -1e30

LANES = 128
MXU_COLS = 256
HEADS_PER_TILE = LANES // HEAD_DIM
N_PAIRS = N_HEADS // HEADS_PER_TILE
LANE_TILES = ATTN_WIDTH // LANES
VMEM_LIMIT = 56 * 1024 * 1024
TOKEN_TILE = 512
ATTN_BLOCKS_PER_STEP = 16
FAST_STRIDE = 4
LOG2E = math.log2(math.e)

F32 = jnp.float32
BF16 = jnp.bfloat16


def _resident(shape, layer=None):
    if layer is None:
        return pl.BlockSpec(shape, lambda *_: (0,) * len(shape),
                            pipeline_mode=pl.Buffered(1))
    return pl.BlockSpec((None,) + tuple(shape), lambda *_: (layer,) + (0,) * len(shape),
                        pipeline_mode=pl.Buffered(1))


def _cast_rider_specs(weights, layer, n_steps):
    in_specs, out_specs, out_shapes = [], [], []
    for w in weights:
        _, rows, cols = w.shape
        slab = rows // n_steps
        assert slab * n_steps == rows and slab % 16 == 0
        in_specs.append(pl.BlockSpec((None, slab, cols), lambda i: (layer, i, 0)))
        out_specs.append(pl.BlockSpec((slab, cols), lambda i: (i, 0)))
        out_shapes.append(jax.ShapeDtypeStruct((rows, cols), BF16))
    return in_specs, out_specs, out_shapes


def _params(n_axes=1):
    return pltpu.CompilerParams(
        dimension_semantics=("arbitrary",) * n_axes, vmem_limit_bytes=VMEM_LIMIT)


def _rms_rows(x, gain):
    ms = jnp.mean(x * x, axis=-1, keepdims=True)
    return x * lax.rsqrt(ms + EPS) * gain


def _conv_layer_kernel(*refs, tm, te, tiles_per_seq, n_riders):
    x_ref, g_ref, win_ref, wc_ref, wout_ref = refs[:5]
    rider_in = refs[5:5 + n_riders]
    o_ref = refs[5 + n_riders]
    rider_out = refs[6 + n_riders:6 + 2 * n_riders]
    h_ref, carry_ref = refs[6 + 2 * n_riders:]
    for src, dst in zip(rider_in, rider_out):
        dst[...] = src[...].astype(BF16)
    i = pl.program_id(0)
    x = x_ref[...]
    h_ref[...] = _rms_rows(x, g_ref[...]).astype(BF16)

    @pl.when(i % tiles_per_seq == 0)
    def _():
        carry_ref[...] = jnp.zeros_like(carry_ref)

    rows = lax.broadcasted_iota(jnp.int32, (tm, te), 0)
    for j in range(CONV_WIDTH // te):
        cols = slice(j * te, (j + 1) * te)

        def proj(p):
            w = win_ref[:, p * CONV_WIDTH + j * te:p * CONV_WIDTH + (j + 1) * te]
            return jnp.dot(h_ref[...], w, preferred_element_type=F32)

        v = proj(1) * proj(2)
        prev1 = carry_ref[1:2, cols]
        prev2 = carry_ref[0:1, cols]
        v1 = jnp.where(rows == 0, prev1, pltpu.roll(v, 1, 0))
        v2 = jnp.where(rows == 0, prev2,
                       jnp.where(rows == 1, prev1, pltpu.roll(v, 2, 0)))
        carry_ref[0:2, cols] = v[tm - 2:tm]
        wc = wc_ref[:, cols]
        conv = wc[0:1] * v2 + wc[1:2] * v1 + wc[2:3] * v
        y = proj(0) * conv * jax.nn.silu(proj(3))
        part = jnp.dot(y.astype(BF16), wout_ref[cols, :],
                       preferred_element_type=F32)
        if j == 0:
            o_ref[...] = x + part
        else:
            o_ref[...] += part


def _conv_layer(x2, norm_g, w_in, w_conv, w_out, layer, seq, next_weights):
    t = x2.shape[0]
    tm, te = 2 * TOKEN_TILE, MXU_COLS
    r_in, r_out, r_shapes = _cast_rider_specs(next_weights, layer, t // tm)
    kern = functools.partial(_conv_layer_kernel, tm=tm, te=te,
                             tiles_per_seq=seq // tm, n_riders=len(next_weights))
    out, *cast = pl.pallas_call(
        kern,
        grid=(t // tm,),
        in_specs=[
            pl.BlockSpec((tm, D_MODEL), lambda i: (i, 0)),
            _resident((1, D_MODEL), layer),
            _resident((D_MODEL, 4 * CONV_WIDTH)),
            _resident((CONV_K, CONV_WIDTH), layer),
            _resident((CONV_WIDTH, D_MODEL)),
        ] + r_in,
        out_specs=[pl.BlockSpec((tm, D_MODEL), lambda i: (i, 0))] + r_out,
        out_shape=[jax.ShapeDtypeStruct((t, D_MODEL), F32)] + r_shapes,
        scratch_shapes=[pltpu.VMEM((tm, D_MODEL), BF16),
                        pltpu.VMEM((8, CONV_WIDTH), F32)],
        compiler_params=_params(),
        name="conv_layer",
    )(x2, norm_g, w_in, w_conv, w_out, *next_weights)
    return out, cast


def _attn_proj_kernel(*refs, tm, n_riders):
    x_ref, g_ref, w_ref, qg_ref, kg_ref = refs[:5]
    rider_in = refs[5:5 + n_riders]
    qkv0_ref, qkv1_ref, qkv2_ref, gate_ref = refs[5 + n_riders:9 + n_riders]
    rider_out = refs[9 + n_riders:9 + 2 * n_riders]
    h_ref, slab_ref = refs[9 + 2 * n_riders:]
    for src, dst in zip(rider_in, rider_out):
        dst[...] = src[...].astype(BF16)
    h32 = _rms_rows(x_ref[...], g_ref[...])
    h_ref[0] = h32.astype(BF16)
    for j in range(LANE_TILES):
        slab_ref[j] = h32[:, j * LANES:(j + 1) * LANES]
    for g in range(1, N_GROUPS):
        r = DILATED_GROUPS[g][1]
        sub = tm // r
        for c in range(r):
            for j in range(LANE_TILES):
                piece = slab_ref[j, pl.ds(c, sub, stride=r), :]
                h_ref[g, c * sub:(c + 1) * sub, j * LANES:(j + 1) * LANES] = piece.astype(BF16)

    z = jnp.dot(h_ref[0], w_ref[:, QKV_COLS:QKV_COLS + ATTN_WIDTH],
                preferred_element_type=F32)
    gate_ref[...] = jax.nn.silu(z).astype(BF16)

    out_refs = (qkv0_ref, qkv1_ref, qkv2_ref)
    low = lax.broadcasted_iota(jnp.int32, (tm, LANES), 1) < HEAD_DIM
    for g in range(N_GROUPS):
        r = DILATED_GROUPS[g][1]
        sub = tm // r
        for part in range(3):
            col0 = (g * 3 + part) * ATTN_WIDTH
            a = jnp.dot(h_ref[g], w_ref[:, col0:col0 + ATTN_WIDTH],
                        preferred_element_type=F32)
            gain_ref = (qg_ref, kg_ref, None)[part]
            for s in range(LANE_TILES):
                sl = slice(s * LANES, (s + 1) * LANES)
                a_s = a[:, sl]
                if gain_ref is not None:
                    sq = a_s * a_s
                    ss_low = jnp.sum(jnp.where(low, sq, 0.0), axis=-1, keepdims=True)
                    ss_high = jnp.sum(jnp.where(low, 0.0, sq), axis=-1, keepdims=True)
                    ss = jnp.where(low, ss_low, ss_high)
                    a_s = a_s * lax.rsqrt(ss * (1.0 / HEAD_DIM) + EPS) * gain_ref[g:g + 1, sl]
                a_s = a_s.astype(BF16)
                for c in range(r):
                    out_refs[g][c, :, part * ATTN_WIDTH + s * LANES:
                                part * ATTN_WIDTH + (s + 1) * LANES] = a_s[c * sub:(c + 1) * sub]


def _dilated_block(r, sub, width, tiles_per_seq):
    return pl.BlockSpec((None, r, sub, width),
                        lambda i: (i // tiles_per_seq, 0, i % tiles_per_seq, 0))


def _attn_proj(x2, norm_g, w_in, q_gain, k_gain, layer, batch, seq, next_weights):
    t = x2.shape[0]
    tm = TOKEN_TILE
    tiles_per_seq = seq // tm
    qg = jnp.tile(q_gain, (1, 1, N_HEADS)) * (HEAD_DIM ** -0.5 * LOG2E)
    kg = jnp.tile(k_gain, (1, 1, N_HEADS))
    qkv_specs, qkv_shapes = [], []
    for _, r in DILATED_GROUPS:
        qkv_specs.append(_dilated_block(r, tm // r, 3 * ATTN_WIDTH, tiles_per_seq))
        qkv_shapes.append(jax.ShapeDtypeStruct((batch, r, seq // r, 3 * ATTN_WIDTH), BF16))
    r_in, r_out, r_shapes = _cast_rider_specs(next_weights, layer + 1, t // tm)
    return pl.pallas_call(
        functools.partial(_attn_proj_kernel, tm=tm, n_riders=len(next_weights)),
        grid=(t // tm,),
        in_specs=[
            pl.BlockSpec((tm, D_MODEL), lambda i: (i, 0)),
            _resident((1, D_MODEL), layer),
            _resident((D_MODEL, QKV_COLS + ATTN_WIDTH)),
            _resident((N_GROUPS, ATTN_WIDTH), layer),
            _resident((N_GROUPS, ATTN_WIDTH), layer),
        ] + r_in,
        out_specs=qkv_specs + [pl.BlockSpec((tm, ATTN_WIDTH), lambda i: (i, 0))] + r_out,
        out_shape=qkv_shapes + [jax.ShapeDtypeStruct((t, ATTN_WIDTH), BF16)] + r_shapes,
        scratch_shapes=[pltpu.VMEM((N_GROUPS, tm, D_MODEL), BF16),
                        pltpu.VMEM((LANE_TILES, tm, LANES), F32)],
        compiler_params=_params(),
        name="attn_proj",
    )(x2, norm_g, w_in, qg, kg, *next_weights)


def _attn_group_kernel(q_ref, k_ref, v_ref, bias_ref, o_ref, stat_ref, *scratch,
                       ncls, nq, carry):
    if carry:
        kprev_ref, vprev_ref = scratch
        n = pl.program_id(2)

        @pl.when(n == 0)
        def _():
            kprev_ref[...] = jnp.zeros_like(kprev_ref)
            vprev_ref[...] = jnp.zeros_like(vprev_ref)

        first = jnp.where(n == 0, 1, 0)
    else:
        first = 1
    lane = lax.broadcasted_iota(jnp.int32, (BLOCK, LANES), 1)
    low = lane < HEAD_DIM
    lane_row = lax.broadcasted_iota(jnp.int32, (1, LANES), 1)
    low_b = (lane_row < HEAD_DIM).astype(F32).astype(BF16)
    high_b = (lane_row >= HEAD_DIM).astype(F32).astype(BF16)
    ones = jnp.ones((2 * BLOCK, LANES), BF16)
    no_prev = jnp.zeros((BLOCK, LANES), BF16)

    for ci in range(ncls):
        for qb in range(nq):
            rows = slice(qb * BLOCK, (qb + 1) * BLOCK)
            stat_tile = jnp.zeros((BLOCK, LANES), F32)
            for pr in range(N_PAIRS):
                cols = slice(pr * LANES, (pr + 1) * LANES)
                q = q_ref[ci, rows, cols]
                q2 = jnp.concatenate([q * low_b, q * high_b], axis=0)
                if qb == 0:
                    kp = kprev_ref[:, cols] if carry else no_prev
                    vp = vprev_ref[:, cols] if carry else no_prev
                    kcat = jnp.concatenate([kp, k_ref[ci, 0:BLOCK, cols]], axis=0)
                    vcat = jnp.concatenate([vp, v_ref[ci, 0:BLOCK, cols]], axis=0)
                    bidx = first
                else:
                    krows = slice((qb - 1) * BLOCK, (qb + 1) * BLOCK)
                    kcat = k_ref[ci, krows, cols]
                    vcat = v_ref[ci, krows, cols]
                    bidx = 0
                s = lax.dot_general(q2, kcat, (((1,), (1,)), ((), ())),
                                    preferred_element_type=F32) + bias_ref[bidx, pr]
                m = jnp.max(s, axis=1, keepdims=True)
                p = jnp.exp2(s - m).astype(BF16)
                oe = jnp.dot(p, jnp.concatenate([vcat, ones], axis=1),
                             preferred_element_type=F32)
                o = jnp.where(low, oe[0:BLOCK, :LANES], oe[BLOCK:2 * BLOCK, :LANES])
                o_ref[ci, rows, cols] = o.astype(BF16)
                l = oe[:, LANES:]
                for hh in range(HEADS_PER_TILE):
                    head = HEADS_PER_TILE * pr + hh
                    hrows = slice(hh * BLOCK, (hh + 1) * BLOCK)
                    stat_tile = jnp.where(lane == head, m[hrows], stat_tile)
                    stat_tile = jnp.where(lane == N_HEADS + head, l[hrows], stat_tile)
            stat_ref[ci, rows, :] = stat_tile

    if carry:
        kprev_ref[...] = k_ref[0, (nq - 1) * BLOCK:nq * BLOCK, :]
        vprev_ref[...] = v_ref[0, (nq - 1) * BLOCK:nq * BLOCK, :]


def _group_bias(rel_bias, g, dilation, span):
    a = jnp.arange(BLOCK)[:, None]
    kk = jnp.arange(2 * BLOCK)[None, :]
    step = BLOCK + a - kk
    valid = (step >= 0) & (step <= span)
    dist = jnp.clip(step, 0, span) * dilation
    max_exact = N_BUCKETS // 2
    d = jnp.maximum(dist, 1).astype(F32)
    large = max_exact + (jnp.log(d / max_exact) / math.log(MAX_DISTANCE / max_exact)
                         * (N_BUCKETS - max_exact)).astype(jnp.int32)
    large = jnp.minimum(large, N_BUCKETS - 1)
    bucket = jnp.where(dist < max_exact, dist, large)
    tab = rel_bias[:, g * N_HEADS:(g + 1) * N_HEADS].astype(F32)
    onehot = (bucket[..., None] == jnp.arange(N_BUCKETS)).astype(F32)
    bias = jnp.einsum("akn,nh->hak", onehot, tab, precision=lax.Precision.HIGHEST) * LOG2E
    interior = jnp.where(valid[None], bias, MASKED)
    first = jnp.where((valid & (kk >= BLOCK))[None], bias, MASKED)
    both = jnp.stack([interior, first])
    return both.reshape(2, N_PAIRS, HEADS_PER_TILE * BLOCK, 2 * BLOCK)


def _attn_group(qkv, rel_bias, g):
    batch, dilation, sub_len, _ = qkv.shape
    span = DILATED_GROUPS[g][0] // dilation
    nq = min(ATTN_BLOCKS_PER_STEP, sub_len // BLOCK)
    rows = nq * BLOCK
    carry = rows < sub_len
    ncls = 1 if carry else min(dilation, ATTN_BLOCKS_PER_STEP // nq)
    bias = _group_bias(rel_bias, g, dilation, span)

    def spec(width, col):
        return pl.BlockSpec((None, ncls, rows, width), lambda b, c, n: (b, c, n, col))

    scratch = [pltpu.VMEM((BLOCK, ATTN_WIDTH), BF16)] * 2 if carry else []
    return pl.pallas_call(
        functools.partial(_attn_group_kernel, ncls=ncls, nq=nq, carry=carry),
        grid=(batch, dilation // ncls, sub_len // rows),
        in_specs=[spec(ATTN_WIDTH, 0), spec(ATTN_WIDTH, 1), spec(ATTN_WIDTH, 2),
                  _resident(bias.shape)],
        out_specs=[spec(ATTN_WIDTH, 0), spec(LANES, 0)],
        out_shape=[jax.ShapeDtypeStruct((batch, dilation, sub_len, ATTN_WIDTH), BF16),
                   jax.ShapeDtypeStruct((batch, dilation, sub_len, LANES), F32)],
        scratch_shapes=scratch,
        compiler_params=_params(3),
        name=f"attn_group{g}",
    )(qkv, qkv, qkv, bias)


def _merge_out_kernel(x_ref, o0_ref, o1_ref, o2_ref, s0_ref, s1_ref, s2_ref,
                      gate_ref, expand_ref, w_ref, out_ref,
                      snat_ref, onat_ref, stage_ref, y_ref, *, tm, chunk):
    o_refs = (o0_ref, o1_ref, o2_ref)
    s_refs = (s0_ref, s1_ref, s2_ref)
    assert DILATED_GROUPS[0][1] == 1
    head_lane = lax.broadcasted_iota(jnp.int32, (chunk, LANES), 1) < N_HEADS
    tiles_per_dot = MXU_COLS // LANES

    for ck in range(tm // chunk):
        row0 = ck * chunk
        nat = slice(row0, row0 + chunk)

        def to_natural(dst, piece, r):
            sub = chunk // r
            src = slice(row0 // r, row0 // r + sub)
            if r <= FAST_STRIDE:
                for c in range(r):
                    dst[pl.ds(row0 + c, sub, stride=r), :] = piece(c, src)
                return
            outer = r // FAST_STRIDE
            for c0 in range(FAST_STRIDE):
                for c1 in range(outer):
                    stage_ref[ck, c0, pl.ds(c1, sub, stride=outer), :] = (
                        piece(c1 * FAST_STRIDE + c0, src))
                dst[pl.ds(row0 + c0, sub * outer, stride=FAST_STRIDE), :] = stage_ref[ck, c0]

        for g in range(1, N_GROUPS):
            r = DILATED_GROUPS[g][1]
            to_natural(snat_ref.at[g - 1], lambda c, src: s_refs[g][c, src, :], r)
            for j in range(LANE_TILES):
                to_natural(onat_ref.at[g - 1, j],
                           lambda c, src: o_refs[g][c, src, j * LANES:(j + 1) * LANES].astype(F32),
                           r)

        def o_tile(g, j):
            if g == 0:
                return o0_ref[0, nat, j * LANES:(j + 1) * LANES].astype(F32)
            return onat_ref[g - 1, j, nat, :]

        stats = [s0_ref[0, nat, :]] + [snat_ref[g - 1, nat, :] for g in range(1, N_GROUPS)]
        m = jnp.maximum(jnp.maximum(stats[0], stats[1]), stats[2])
        es = [jnp.exp2(st - m) for st in stats]
        den = sum(e * pltpu.roll(st, LANES - N_HEADS, 1) for e, st in zip(es, stats))
        inv = 1.0 / jnp.where(head_lane, den, 1.0)
        weights = []
        for e in es:
            alpha = jnp.where(head_lane, e * inv, 0.0)
            hi = alpha.astype(BF16)
            lo = (alpha - hi.astype(F32)).astype(BF16)
            weights.append(jnp.concatenate([hi, lo], axis=1))
        for jj in range(LANE_TILES // tiles_per_dot):
            spread = [jnp.dot(w, expand_ref[:, jj * MXU_COLS:(jj + 1) * MXU_COLS],
                              preferred_element_type=F32) for w in weights]
            for jt in range(tiles_per_dot):
                j = jj * tiles_per_dot + jt
                cols = slice(j * LANES, (j + 1) * LANES)
                o = sum(spread[g][:, jt * LANES:(jt + 1) * LANES] * o_tile(g, j)
                        for g in range(N_GROUPS))
                y_ref[nat, cols] = (o * gate_ref[nat, cols].astype(F32)).astype(BF16)
        out_ref[nat, :] = x_ref[nat, :] + jnp.dot(y_ref[nat, :], w_ref[...],
                                                  preferred_element_type=F32)


def _merge_out(x2, outs, stats, gate, w_out, seq):
    t = x2.shape[0]
    tm = 2 * TOKEN_TILE
    tiles_per_seq = seq // tm
    lane_head = jnp.arange(2 * LANES) % LANES
    expand = (lane_head[:, None] == (jnp.arange(ATTN_WIDTH) // HEAD_DIM)[None, :]).astype(BF16)
    row = lambda w: pl.BlockSpec((tm, w), lambda i: (i, 0))
    o_specs = [_dilated_block(r, tm // r, ATTN_WIDTH, tiles_per_seq) for _, r in DILATED_GROUPS]
    l_specs = [_dilated_block(r, tm // r, LANES, tiles_per_seq) for _, r in DILATED_GROUPS]
    return pl.pallas_call(
        functools.partial(_merge_out_kernel, tm=tm, chunk=TOKEN_TILE),
        grid=(t // tm,),
        in_specs=[row(D_MODEL)] + o_specs + l_specs
                 + [row(ATTN_WIDTH), _resident(expand.shape),
                    _resident((ATTN_WIDTH, D_MODEL))],
        out_specs=row(D_MODEL),
        out_shape=jax.ShapeDtypeStruct((t, D_MODEL), F32),
        scratch_shapes=[pltpu.VMEM((N_GROUPS - 1, tm, LANES), F32),
                        pltpu.VMEM((N_GROUPS - 1, LANE_TILES, tm, LANES), F32),
                        pltpu.VMEM((tm // TOKEN_TILE, FAST_STRIDE, TOKEN_TILE // FAST_STRIDE, LANES), F32),
                        pltpu.VMEM((tm, ATTN_WIDTH), BF16)],
        compiler_params=_params(),
        name="merge_out",
    )(x2, *outs, *stats, gate, expand, w_out)


def _attn_layer(x2, norm_g, w_in, q_gain, k_gain, w_out, rel_bias, layer, batch, seq,
                next_weights):
    qkv0, qkv1, qkv2, gate, *cast = _attn_proj(x2, norm_g, w_in, q_gain, k_gain, layer,
                                               batch, seq, next_weights)
    outs, stats = [], []
    for g, qkv in enumerate((qkv0, qkv1, qkv2)):
        o, st = _attn_group(qkv, rel_bias, g)
        outs.append(o)
        stats.append(st)
    return _merge_out(x2, outs, stats, gate, w_out, seq), cast


def kernel(x, conv_norm, conv_w_in, conv_w, conv_w_out, attn_norm, attn_w_in,
           attn_q_gain, attn_k_gain, attn_w_out, rel_bias):
    batch, seq, d = x.shape
    assert d == D_MODEL and seq % TOKEN_TILE == 0
    assert all(TOKEN_TILE % (8 * r) == 0 and seq % (BLOCK * r) == 0
               for _, r in DILATED_GROUPS)
    depth = conv_norm.shape[0] + attn_norm.shape[0]
    conv_norm3, attn_norm3 = conv_norm[:, None, :], attn_norm[:, None, :]
    w_in, w_out = conv_w_in[0].astype(BF16), conv_w_out[0].astype(BF16)
    x2 = x.reshape(batch * seq, d)
    for i in range(depth):
        j = i // 2
        if i % 2 == 0:
            nxt = (attn_w_in, attn_w_out) if i + 1 < depth else ()
            x2, cast = _conv_layer(x2, conv_norm3, w_in, conv_w, w_out, j, seq, nxt)
        else:
            nxt = (conv_w_in, conv_w_out) if i + 1 < depth else ()
            x2, cast = _attn_layer(x2, attn_norm3, w_in, attn_q_gain, attn_k_gain, w_out,
                                   rel_bias, j, batch, seq, nxt)
        if cast:
            w_in, w_out = cast
    return x2.reshape(batch, seq, d)
```

```python
import functools
import math

import jax
import jax.numpy as jnp
from jax import lax
from jax.experimental import pallas as pl
from jax.experimental.pallas import tpu as pltpu

D_MODEL = 1024
CONV_WIDTH = 2048
CONV_K = 3
HEAD_DIM = 64
N_HEADS = 16
ATTN_WIDTH = N_HEADS * HEAD_DIM
DILATED_GROUPS = ((128, 1), (512, 4), (2048, 16))
N_GROUPS = len(DILATED_GROUPS)
QKV_COLS = N_GROUPS * 3 * ATTN_WIDTH
BLOCK = 128
N_BUCKETS = 32
MAX_DISTANCE = 2048
EPS = 1e-6
MASKED = -1e30

LANES = 128
F32_SUBLANES = 8
BF16_SUBLANES = 16
MXU_COLS = 256
HEADS_PER_TILE = LANES // HEAD_DIM
N_PAIRS = N_HEADS // HEADS_PER_TILE
LANE_TILES = ATTN_WIDTH // LANES
VMEM_LIMIT = 56 * 1024 * 1024
TOKEN_TILE = 512
ATTN_BLOCKS_PER_STEP = 16
FAST_STRIDE = 4
LOG2E = math.log2(math.e)

F32 = jnp.float32
BF16 = jnp.bfloat16


def _resident(shape, layer=None):
    if layer is None:
        return pl.BlockSpec(shape, lambda *_: (0,) * len(shape),
                            pipeline_mode=pl.Buffered(1))
    return pl.BlockSpec((None,) + tuple(shape), lambda *_: (layer,) + (0,) * len(shape),
                        pipeline_mode=pl.Buffered(1))


def _cast_rider_specs(weights, layer, n_steps):
    in_specs, out_specs, out_shapes = [], [], []
    for w in weights:
        _, rows, cols = w.shape
        slab = rows // n_steps
        assert slab * n_steps == rows and slab % BF16_SUBLANES == 0
        in_specs.append(pl.BlockSpec((None, slab, cols), lambda i: (layer, i, 0)))
        out_specs.append(pl.BlockSpec((slab, cols), lambda i: (i, 0)))
        out_shapes.append(jax.ShapeDtypeStruct((rows, cols), BF16))
    return in_specs, out_specs, out_shapes


def _params(n_axes=1):
    return pltpu.CompilerParams(
        dimension_semantics=("arbitrary",) * n_axes, vmem_limit_bytes=VMEM_LIMIT)


def _rms_rows(x, gain):
    ms = jnp.mean(x * x, axis=-1, keepdims=True)
    return x * lax.rsqrt(ms + EPS) * gain


def _conv_layer_kernel(*refs, tm, te, tiles_per_seq, n_riders):
    x_ref, g_ref, win_ref, wc_ref, wout_ref = refs[:5]
    rider_in = refs[5:5 + n_riders]
    o_ref = refs[5 + n_riders]
    rider_out = refs[6 + n_riders:6 + 2 * n_riders]
    h_ref, carry_ref = refs[6 + 2 * n_riders:]
    for src, dst in zip(rider_in, rider_out):
        dst[...] = src[...].astype(BF16)
    i = pl.program_id(0)
    x = x_ref[...]
    h_ref[...] = _rms_rows(x, g_ref[...]).astype(BF16)

    @pl.when(i % tiles_per_seq == 0)
    def _():
        carry_ref[...] = jnp.zeros_like(carry_ref)

    rows = lax.broadcasted_iota(jnp.int32, (tm, te), 0)
    for j in range(CONV_WIDTH // te):
        cols = slice(j * te, (j + 1) * te)

        def proj(p):
            w = win_ref[:, p * CONV_WIDTH + j * te:p * CONV_WIDTH + (j + 1) * te]
            return jnp.dot(h_ref[...], w, preferred_element_type=F32)

        v = proj(1) * proj(2)
        prev1 = carry_ref[1:2, cols]
        prev2 = carry_ref[0:1, cols]
        v1 = jnp.where(rows == 0, prev1, pltpu.roll(v, 1, 0))
        v2 = jnp.where(rows == 0, prev2,
                       jnp.where(rows == 1, prev1, pltpu.roll(v, 2, 0)))
        carry_ref[0:2, cols] = v[tm - 2:tm]
        wc = wc_ref[:, cols]
        conv = wc[0:1] * v2 + wc[1:2] * v1 + wc[2:3] * v
        y = proj(0) * conv * jax.nn.silu(proj(3))
        part = jnp.dot(y.astype(BF16), wout_ref[cols, :],
                       preferred_element_type=F32)
        if j == 0:
            o_ref[...] = x + part
        else:
            o_ref[...] += part


def _conv_layer(x2, norm_g, w_in, w_conv, w_out, layer, seq, next_weights):
    t = x2.shape[0]
    tm, te = 2 * TOKEN_TILE, MXU_COLS
    r_in, r_out, r_shapes = _cast_rider_specs(next_weights, layer, t // tm)
    kern = functools.partial(_conv_layer_kernel, tm=tm, te=te,
                             tiles_per_seq=seq // tm, n_riders=len(next_weights))
    out, *cast = pl.pallas_call(
        kern,
        grid=(t // tm,),
        in_specs=[
            pl.BlockSpec((tm, D_MODEL), lambda i: (i, 0)),
            _resident((1, D_MODEL), layer),
            _resident((D_MODEL, 4 * CONV_WIDTH)),
            _resident((CONV_K, CONV_WIDTH), layer),
            _resident((CONV_WIDTH, D_MODEL)),
        ] + r_in,
        out_specs=[pl.BlockSpec((tm, D_MODEL), lambda i: (i, 0))] + r_out,
        out_shape=[jax.ShapeDtypeStruct((t, D_MODEL), F32)] + r_shapes,
        scratch_shapes=[pltpu.VMEM((tm, D_MODEL), BF16),
                        pltpu.VMEM((F32_SUBLANES, CONV_WIDTH), F32)],
        compiler_params=_params(),
        name="conv_layer",
    )(x2, norm_g, w_in, w_conv, w_out, *next_weights)
    return out, cast


def _attn_proj_kernel(*refs, tm, n_riders):
    x_ref, g_ref, w_ref, qg_ref, kg_ref = refs[:5]
    rider_in = refs[5:5 + n_riders]
    qkv0_ref, qkv1_ref, qkv2_ref, gate_ref = refs[5 + n_riders:9 + n_riders]
    rider_out = refs[9 + n_riders:9 + 2 * n_riders]
    h_ref, slab_ref = refs[9 + 2 * n_riders:]
    for src, dst in zip(rider_in, rider_out):
        dst[...] = src[...].astype(BF16)
    h32 = _rms_rows(x_ref[...], g_ref[...])
    h_ref[0] = h32.astype(BF16)
    for j in range(LANE_TILES):
        slab_ref[j] = h32[:, j * LANES:(j + 1) * LANES]
    for g in range(1, N_GROUPS):
        r = DILATED_GROUPS[g][1]
        sub = tm // r
        for c in range(r):
            for j in range(LANE_TILES):
                piece = slab_ref[j, pl.ds(c, sub, stride=r), :]
                h_ref[g, c * sub:(c + 1) * sub, j * LANES:(j + 1) * LANES] = piece.astype(BF16)

    z = jnp.dot(h_ref[0], w_ref[:, QKV_COLS:QKV_COLS + ATTN_WIDTH],
                preferred_element_type=F32)
    gate_ref[...] = jax.nn.silu(z).astype(BF16)

    out_refs = (qkv0_ref, qkv1_ref, qkv2_ref)
    low = lax.broadcasted_iota(jnp.int32, (tm, LANES), 1) < HEAD_DIM
    for g in range(N_GROUPS):
        r = DILATED_GROUPS[g][1]
        sub = tm // r
        for part in range(3):
            col0 = (g * 3 + part) * ATTN_WIDTH
            a = jnp.dot(h_ref[g], w_ref[:, col0:col0 + ATTN_WIDTH],
                        preferred_element_type=F32)
            gain_ref = (qg_ref, kg_ref, None)[part]
            for s in range(LANE_TILES):
                sl = slice(s * LANES, (s + 1) * LANES)
                a_s = a[:, sl]
                if gain_ref is not None:
                    sq = a_s * a_s
                    ss_low = jnp.sum(jnp.where(low, sq, 0.0), axis=-1, keepdims=True)
                    ss_high = jnp.sum(jnp.where(low, 0.0, sq), axis=-1, keepdims=True)
                    ss = jnp.where(low, ss_low, ss_high)
                    a_s = a_s * lax.rsqrt(ss * (1.0 / HEAD_DIM) + EPS) * gain_ref[g:g + 1, sl]
                a_s = a_s.astype(BF16)
                for c in range(r):
                    out_refs[g][c, :, part * ATTN_WIDTH + s * LANES:
                                part * ATTN_WIDTH + (s + 1) * LANES] = a_s[c * sub:(c + 1) * sub]


def _dilated_block(r, sub, width, tiles_per_seq):
    return pl.BlockSpec((None, r, sub, width),
                        lambda i: (i // tiles_per_seq, 0, i % tiles_per_seq, 0))


def _attn_proj(x2, norm_g, w_in, q_gain, k_gain, layer, batch, seq, next_weights):
    t = x2.shape[0]
    tm = TOKEN_TILE
    tiles_per_seq = seq // tm
    qg = jnp.tile(q_gain, (1, 1, N_HEADS)) * (HEAD_DIM ** -0.5 * LOG2E)
    kg = jnp.tile(k_gain, (1, 1, N_HEADS))
    qkv_specs, qkv_shapes = [], []
    for _, r in DILATED_GROUPS:
        qkv_specs.append(_dilated_block(r, tm // r, 3 * ATTN_WIDTH, tiles_per_seq))
        qkv_shapes.append(jax.ShapeDtypeStruct((batch, r, seq // r, 3 * ATTN_WIDTH), BF16))
    r_in, r_out, r_shapes = _cast_rider_specs(next_weights, layer + 1, t // tm)
    return pl.pallas_call(
        functools.partial(_attn_proj_kernel, tm=tm, n_riders=len(next_weights)),
        grid=(t // tm,),
        in_specs=[
            pl.BlockSpec((tm, D_MODEL), lambda i: (i, 0)),
            _resident((1, D_MODEL), layer),
            _resident((D_MODEL, QKV_COLS + ATTN_WIDTH)),
            _resident((N_GROUPS, ATTN_WIDTH), layer),
            _resident((N_GROUPS, ATTN_WIDTH), layer),
        ] + r_in,
        out_specs=qkv_specs + [pl.BlockSpec((tm, ATTN_WIDTH), lambda i: (i, 0))] + r_out,
        out_shape=qkv_shapes + [jax.ShapeDtypeStruct((t, ATTN_WIDTH), BF16)] + r_shapes,
        scratch_shapes=[pltpu.VMEM((N_GROUPS, tm, D_MODEL), BF16),
                        pltpu.VMEM((LANE_TILES, tm, LANES), F32)],
        compiler_params=_params(),
        name="attn_proj",
    )(x2, norm_g, w_in, qg, kg, *next_weights)


def _attn_group_kernel(q_ref, k_ref, v_ref, bias_ref, o_ref, stat_ref, *scratch,
                       ncls, nq, carry):
    if carry:
        kprev_ref, vprev_ref = scratch
        n = pl.program_id(2)

        @pl.when(n == 0)
        def _():
            kprev_ref[...] = jnp.zeros_like(kprev_ref)
            vprev_ref[...] = jnp.zeros_like(vprev_ref)

        first = jnp.where(n == 0, 1, 0)
    else:
        first = 1
    lane = lax.broadcasted_iota(jnp.int32, (BLOCK, LANES), 1)
    low = lane < HEAD_DIM
    lane_row = lax.broadcasted_iota(jnp.int32, (1, LANES), 1)
    low_b = (lane_row < HEAD_DIM).astype(F32).astype(BF16)
    high_b = (lane_row >= HEAD_DIM).astype(F32).astype(BF16)
    ones = jnp.ones((2 * BLOCK, LANES), BF16)
    no_prev = jnp.zeros((BLOCK, LANES), BF16)

    for ci in range(ncls):
        for qb in range(nq):
            rows = slice(qb * BLOCK, (qb + 1) * BLOCK)
            stat_tile = jnp.zeros((BLOCK, LANES), F32)
            for pr in range(N_PAIRS):
                cols = slice(pr * LANES, (pr + 1) * LANES)
                q = q_ref[ci, rows, cols]
                q2 = jnp.concatenate([q * low_b, q * high_b], axis=0)
                if qb == 0:
                    kp = kprev_ref[:, cols] if carry else no_prev
                    vp = vprev_ref[:, cols] if carry else no_prev
                    kcat = jnp.concatenate([kp, k_ref[ci, 0:BLOCK, cols]], axis=0)
                    vcat = jnp.concatenate([vp, v_ref[ci, 0:BLOCK, cols]], axis=0)
                    bidx = first
                else:
                    krows = slice((qb - 1) * BLOCK, (qb + 1) * BLOCK)
                    kcat = k_ref[ci, krows, cols]
                    vcat = v_ref[ci, krows, cols]
                    bidx = 0
                s = lax.dot_general(q2, kcat, (((1,), (1,)), ((), ())),
                                    preferred_element_type=F32) + bias_ref[bidx, pr]
                m = jnp.max(s, axis=1, keepdims=True)
                p = jnp.exp2(s - m).astype(BF16)
                oe = jnp.dot(p, jnp.concatenate([vcat, ones], axis=1),
                             preferred_element_type=F32)
                o = jnp.where(low, oe[0:BLOCK, :LANES], oe[BLOCK:2 * BLOCK, :LANES])
                o_ref[ci, rows, cols] = o.astype(BF16)
                l = oe[:, LANES:]
                for hh in range(HEADS_PER_TILE):
                    head = HEADS_PER_TILE * pr + hh
                    hrows = slice(hh * BLOCK, (hh + 1) * BLOCK)
                    stat_tile = jnp.where(lane == head, m[hrows], stat_tile)
                    stat_tile = jnp.where(lane == N_HEADS + head, l[hrows], stat_tile)
            stat_ref[ci, rows, :] = stat_tile

    if carry:
        kprev_ref[...] = k_ref[0, (nq - 1) * BLOCK:nq * BLOCK, :]
        vprev_ref[...] = v_ref[0, (nq - 1) * BLOCK:nq * BLOCK, :]


def _band_bias(rel_bias):
    span = BLOCK
    assert all(window == span * dilation for window, dilation in DILATED_GROUPS)
    a = jnp.arange(BLOCK)[:, None]
    kk = jnp.arange(2 * BLOCK)[None, :]
    step = BLOCK + a - kk
    valid = (step >= 0) & (step <= span)
    dilations = jnp.array([r for _, r in DILATED_GROUPS])[:, None, None]
    dist = jnp.clip(step, 0, span)[None] * dilations
    max_exact = N_BUCKETS // 2
    d = jnp.maximum(dist, 1).astype(F32)
    large = max_exact + (jnp.log(d / max_exact) / math.log(MAX_DISTANCE / max_exact)
                         * (N_BUCKETS - max_exact)).astype(jnp.int32)
    large = jnp.minimum(large, N_BUCKETS - 1)
    bucket = jnp.where(dist < max_exact, dist, large)
    tab = rel_bias.astype(F32).reshape(N_BUCKETS, N_GROUPS, N_HEADS)
    onehot = (bucket[..., None] == jnp.arange(N_BUCKETS)).astype(F32)
    bias = jnp.einsum("gakn,ngh->ghak", onehot, tab,
                      precision=lax.Precision.HIGHEST) * LOG2E
    interior = jnp.where(valid, bias, MASKED)
    first = jnp.where(valid & (kk >= BLOCK), bias, MASKED)
    both = jnp.stack([interior, first], axis=1)
    return both.reshape(N_GROUPS, 2, N_PAIRS, HEADS_PER_TILE * BLOCK, 2 * BLOCK)


def _attn_group(qkv, bias, g):
    batch, dilation, sub_len, _ = qkv.shape
    nq = min(ATTN_BLOCKS_PER_STEP, sub_len // BLOCK)
    rows = nq * BLOCK
    carry = rows < sub_len
    ncls = 1 if carry else min(dilation, ATTN_BLOCKS_PER_STEP // nq)

    def spec(width, col):
        return pl.BlockSpec((None, ncls, rows, width), lambda b, c, n: (b, c, n, col))

    scratch = [pltpu.VMEM((BLOCK, ATTN_WIDTH), BF16)] * 2 if carry else []
    return pl.pallas_call(
        functools.partial(_attn_group_kernel, ncls=ncls, nq=nq, carry=carry),
        grid=(batch, dilation // ncls, sub_len // rows),
        in_specs=[spec(ATTN_WIDTH, 0), spec(ATTN_WIDTH, 1), spec(ATTN_WIDTH, 2),
                  _resident(bias.shape[1:], g)],
        out_specs=[spec(ATTN_WIDTH, 0), spec(LANES, 0)],
        out_shape=[jax.ShapeDtypeStruct((batch, dilation, sub_len, ATTN_WIDTH), BF16),
                   jax.ShapeDtypeStruct((batch, dilation, sub_len, LANES), F32)],
        scratch_shapes=scratch,
        compiler_params=_params(3),
        name=f"attn_group{g}",
    )(qkv, qkv, qkv, bias)


def _merge_out_kernel(x_ref, o0_ref, o1_ref, o2_ref, s0_ref, s1_ref, s2_ref,
                      gate_ref, expand_ref, w_ref, out_ref,
                      snat_ref, onat_ref, stage_ref, y_ref, *, tm, chunk):
    o_refs = (o0_ref, o1_ref, o2_ref)
    s_refs = (s0_ref, s1_ref, s2_ref)
    assert DILATED_GROUPS[0][1] == 1
    head_lane = lax.broadcasted_iota(jnp.int32, (chunk, LANES), 1) < N_HEADS
    tiles_per_dot = MXU_COLS // LANES

    for ck in range(tm // chunk):
        row0 = ck * chunk
        nat = slice(row0, row0 + chunk)

        def to_natural(dst, piece, r):
            sub = chunk // r
            src = slice(row0 // r, row0 // r + sub)
            if r <= FAST_STRIDE:
                for c in range(r):
                    dst[pl.ds(row0 + c, sub, stride=r), :] = piece(c, src)
                return
            outer = r // FAST_STRIDE
            for c0 in range(FAST_STRIDE):
                for c1 in range(outer):
                    stage_ref[ck, c0, pl.ds(c1, sub, stride=outer), :] = (
                        piece(c1 * FAST_STRIDE + c0, src))
                dst[pl.ds(row0 + c0, sub * outer, stride=FAST_STRIDE), :] = stage_ref[ck, c0]

        for g in range(1, N_GROUPS):
            r = DILATED_GROUPS[g][1]
            to_natural(snat_ref.at[g - 1], lambda c, src: s_refs[g][c, src, :], r)
            for j in range(LANE_TILES):
                to_natural(onat_ref.at[g - 1, j],
                           lambda c, src: o_refs[g][c, src, j * LANES:(j + 1) * LANES].astype(F32),
                           r)

        def o_tile(g, j):
            if g == 0:
                return o0_ref[0, nat, j * LANES:(j + 1) * LANES].astype(F32)
            return onat_ref[g - 1, j, nat, :]

        stats = [s0_ref[0, nat, :]] + [snat_ref[g - 1, nat, :] for g in range(1, N_GROUPS)]
        m = jnp.maximum(jnp.maximum(stats[0], stats[1]), stats[2])
        es = [jnp.exp2(st - m) for st in stats]
        den = sum(e * pltpu.roll(st, LANES - N_HEADS, 1) for e, st in zip(es, stats))
        inv = 1.0 / jnp.where(head_lane, den, 1.0)
        weights = []
        for e in es:
            alpha = jnp.where(head_lane, e * inv, 0.0)
            hi = alpha.astype(BF16)
            lo = (alpha - hi.astype(F32)).astype(BF16)
            weights.append(jnp.concatenate([hi, lo], axis=1))
        for jj in range(LANE_TILES // tiles_per_dot):
            spread = [jnp.dot(w, expand_ref[:, jj * MXU_COLS:(jj + 1) * MXU_COLS],
                              preferred_element_type=F32) for w in weights]
            for jt in range(tiles_per_dot):
                j = jj * tiles_per_dot + jt
                cols = slice(j * LANES, (j + 1) * LANES)
                o = sum(spread[g][:, jt * LANES:(jt + 1) * LANES] * o_tile(g, j)
                        for g in range(N_GROUPS))
                y_ref[nat, cols] = (o * gate_ref[nat, cols].astype(F32)).astype(BF16)
        out_ref[nat, :] = x_ref[nat, :] + jnp.dot(y_ref[nat, :], w_ref[...],
                                                  preferred_element_type=F32)


def _merge_out(x2, outs, stats, gate, w_out, seq):
    t = x2.shape[0]
    tm = 2 * TOKEN_TILE
    tiles_per_seq = seq // tm
    lane_head = jnp.arange(2 * LANES) % LANES
    expand = (lane_head[:, None] == (jnp.arange(ATTN_WIDTH) // HEAD_DIM)[None, :]).astype(BF16)
    row = lambda w: pl.BlockSpec((tm, w), lambda i: (i, 0))
    o_specs = [_dilated_block(r, tm // r, ATTN_WIDTH, tiles_per_seq) for _, r in DILATED_GROUPS]
    l_specs = [_dilated_block(r, tm // r, LANES, tiles_per_seq) for _, r in DILATED_GROUPS]
    return pl.pallas_call(
        functools.partial(_merge_out_kernel, tm=tm, chunk=TOKEN_TILE),
        grid=(t // tm,),
        in_specs=[row(D_MODEL)] + o_specs + l_specs
                 + [row(ATTN_WIDTH), _resident(expand.shape),
                    _resident((ATTN_WIDTH, D_MODEL))],
        out_specs=row(D_MODEL),
        out_shape=jax.ShapeDtypeStruct((t, D_MODEL), F32),
        scratch_shapes=[pltpu.VMEM((N_GROUPS - 1, tm, LANES), F32),
                        pltpu.VMEM((N_GROUPS - 1, LANE_TILES, tm, LANES), F32),
                        pltpu.VMEM((tm // TOKEN_TILE, FAST_STRIDE, TOKEN_TILE // FAST_STRIDE, LANES), F32),
                        pltpu.VMEM((tm, ATTN_WIDTH), BF16)],
        compiler_params=_params(),
        name="merge_out",
    )(x2, *outs, *stats, gate, expand, w_out)


def _attn_layer(x2, norm_g, w_in, q_gain, k_gain, w_out, band_bias, layer, batch, seq,
                next_weights):
    qkv0, qkv1, qkv2, gate, *cast = _attn_proj(x2, norm_g, w_in, q_gain, k_gain, layer,
                                               batch, seq, next_weights)
    outs, stats = [], []
    for g, qkv in enumerate((qkv0, qkv1, qkv2)):
        o, st = _attn_group(qkv, band_bias, g)
        outs.append(o)
        stats.append(st)
    return _merge_out(x2, outs, stats, gate, w_out, seq), cast


def kernel(x, conv_norm, conv_w_in, conv_w, conv_w_out, attn_norm, attn_w_in,
           attn_q_gain, attn_k_gain, attn_w_out, rel_bias):
    batch, seq, d = x.shape
    assert d == D_MODEL and seq % TOKEN_TILE == 0
    assert all(TOKEN_TILE % (BF16_SUBLANES * r) == 0 and seq % (BLOCK * r) == 0
               for _, r in DILATED_GROUPS)
    depth = conv_norm.shape[0] + attn_norm.shape[0]
    conv_norm3, attn_norm3 = conv_norm[:, None, :], attn_norm[:, None, :]
    w_in, w_out = conv_w_in[0].astype(BF16), conv_w_out[0].astype(BF16)
    band_bias = _band_bias(rel_bias)
    x2 = x.reshape(batch * seq, d)
    for i in range(depth):
        j = i // 2
        if i % 2 == 0:
            nxt = (attn_w_in, attn_w_out) if i + 1 < depth else ()
            x2, cast = _conv_layer(x2, conv_norm3, w_in, conv_w, w_out, j, seq, nxt)
        else:
            nxt = (conv_w_in, conv_w_out) if i + 1 < depth else ()
            x2, cast = _attn_layer(x2, attn_norm3, w_in, attn_q_gain, attn_k_gain, w_out,
                                   band_bias, j, batch, seq, nxt)
        if cast:
            w_in, w_out = cast
    return x2.reshape(batch, seq, d)
```

```python
import functools
import math

import jax
import jax.numpy as jnp
from jax import lax
from jax.experimental import pallas as pl
from jax.experimental.pallas import tpu as pltpu

D_MODEL = 1024
CONV_WIDTH = 2048
CONV_K = 3
HEAD_DIM = 64
N_HEADS = 16
ATTN_WIDTH = N_HEADS * HEAD_DIM
DILATED_GROUPS = ((128, 1), (512, 4), (2048, 16))
N_GROUPS = len(DILATED_GROUPS)
QKV_COLS = N_GROUPS * 3 * ATTN_WIDTH
BLOCK = 128
N_BUCKETS = 32
MAX_DISTANCE = 2048
EPS = 1e-6
MASKED = -1e30

LANES = 128
F32_SUBLANES = 8
BF16_SUBLANES = 16
MXU_COLS = 256
HEADS_PER_TILE = LANES // HEAD_DIM
N_PAIRS = N_HEADS // HEADS_PER_TILE
LANE_TILES = ATTN_WIDTH // LANES
VMEM_LIMIT = 56 * 1024 * 1024
TOKEN_TILE = 512
ATTN_BLOCKS_PER_STEP = 16
FAST_STRIDE = 4
LOG2E = math.log2(math.e)

F32 = jnp.float32
BF16 = jnp.bfloat16


def _resident(shape, layer=None):
    if layer is None:
        return pl.BlockSpec(shape, lambda *_: (0,) * len(shape),
                            pipeline_mode=pl.Buffered(1))
    return pl.BlockSpec((None,) + tuple(shape), lambda *_: (layer,) + (0,) * len(shape),
                        pipeline_mode=pl.Buffered(1))


def _cast_rider_specs(weights, layer, n_steps):
    in_specs, out_specs, out_shapes = [], [], []
    for w in weights:
        _, rows, cols = w.shape
        slab = rows // n_steps
        assert slab * n_steps == rows and slab % BF16_SUBLANES == 0
        in_specs.append(pl.BlockSpec((None, slab, cols), lambda i: (layer, i, 0)))
        out_specs.append(pl.BlockSpec((slab, cols), lambda i: (i, 0)))
        out_shapes.append(jax.ShapeDtypeStruct((rows, cols), BF16))
    return in_specs, out_specs, out_shapes


def _params(n_axes=1):
    return pltpu.CompilerParams(
        dimension_semantics=("arbitrary",) * n_axes, vmem_limit_bytes=VMEM_LIMIT)


def _rms_rows(x, gain):
    ms = jnp.mean(x * x, axis=-1, keepdims=True)
    return x * lax.rsqrt(ms + EPS) * gain


def _conv_layer_kernel(*refs, tm, te, tiles_per_seq, n_riders):
    x_ref, g_ref, win_ref, wc_ref, wout_ref = refs[:5]
    rider_in = refs[5:5 + n_riders]
    o_ref = refs[5 + n_riders]
    rider_out = refs[6 + n_riders:6 + 2 * n_riders]
    h_ref, carry_ref = refs[6 + 2 * n_riders:]
    for src, dst in zip(rider_in, rider_out):
        dst[...] = src[...].astype(BF16)
    i = pl.program_id(0)
    x = x_ref[...]
    h_ref[...] = _rms_rows(x, g_ref[...]).astype(BF16)

    @pl.when(i % tiles_per_seq == 0)
    def _():
        carry_ref[...] = jnp.zeros_like(carry_ref)

    rows = lax.broadcasted_iota(jnp.int32, (tm, te), 0)
    for j in range(CONV_WIDTH // te):
        cols = slice(j * te, (j + 1) * te)

        def proj(p):
            w = win_ref[:, p * CONV_WIDTH + j * te:p * CONV_WIDTH + (j + 1) * te]
            return jnp.dot(h_ref[...], w, preferred_element_type=F32)

        v = proj(1) * proj(2)
        prev1 = carry_ref[1:2, cols]
        prev2 = carry_ref[0:1, cols]
        v1 = jnp.where(rows == 0, prev1, pltpu.roll(v, 1, 0))
        v2 = jnp.where(rows == 0, prev2,
                       jnp.where(rows == 1, prev1, pltpu.roll(v, 2, 0)))
        carry_ref[0:2, cols] = v[tm - 2:tm]
        wc = wc_ref[:, cols]
        conv = wc[0:1] * v2 + wc[1:2] * v1 + wc[2:3] * v
        y = proj(0) * conv * jax.nn.silu(proj(3))
        part = jnp.dot(y.astype(BF16), wout_ref[cols, :],
                       preferred_element_type=F32)
        if j == 0:
            o_ref[...] = x + part
        else:
            o_ref[...] += part


def _conv_layer(x2, norm_g, w_in, w_conv, w_out, layer, seq, next_weights):
    t = x2.shape[0]
    tm, te = 2 * TOKEN_TILE, MXU_COLS
    r_in, r_out, r_shapes = _cast_rider_specs(next_weights, layer, t // tm)
    kern = functools.partial(_conv_layer_kernel, tm=tm, te=te,
                             tiles_per_seq=seq // tm, n_riders=len(next_weights))
    out, *cast = pl.pallas_call(
        kern,
        grid=(t // tm,),
        in_specs=[
            pl.BlockSpec((tm, D_MODEL), lambda i: (i, 0)),
            _resident((1, D_MODEL), layer),
            _resident((D_MODEL, 4 * CONV_WIDTH)),
            _resident((CONV_K, CONV_WIDTH), layer),
            _resident((CONV_WIDTH, D_MODEL)),
        ] + r_in,
        out_specs=[pl.BlockSpec((tm, D_MODEL), lambda i: (i, 0))] + r_out,
        out_shape=[jax.ShapeDtypeStruct((t, D_MODEL), F32)] + r_shapes,
        scratch_shapes=[pltpu.VMEM((tm, D_MODEL), BF16),
                        pltpu.VMEM((F32_SUBLANES, CONV_WIDTH), F32)],
        compiler_params=_params(),
        name="conv_layer",
    )(x2, norm_g, w_in, w_conv, w_out, *next_weights)
    return out, cast


def _attn_proj_kernel(*refs, tm, n_riders):
    x_ref, g_ref, w_ref, qg_ref, kg_ref = refs[:5]
    rider_in = refs[5:5 + n_riders]
    qkv0_ref, qkv1_ref, qkv2_ref, gate_ref = refs[5 + n_riders:9 + n_riders]
    rider_out = refs[9 + n_riders:9 + 2 * n_riders]
    h_ref, slab_ref = refs[9 + 2 * n_riders:]
    for src, dst in zip(rider_in, rider_out):
        dst[...] = src[...].astype(BF16)
    h32 = _rms_rows(x_ref[...], g_ref[...])
    h_ref[0] = h32.astype(BF16)
    for j in range(LANE_TILES):
        slab_ref[j] = h32[:, j * LANES:(j + 1) * LANES]
    for g in range(1, N_GROUPS):
        r = DILATED_GROUPS[g][1]
        sub = tm // r
        for c in range(r):
            for j in range(LANE_TILES):
                piece = slab_ref[j, pl.ds(c, sub, stride=r), :]
                h_ref[g, c * sub:(c + 1) * sub, j * LANES:(j + 1) * LANES] = piece.astype(BF16)

    z = jnp.dot(h_ref[0], w_ref[:, QKV_COLS:QKV_COLS + ATTN_WIDTH],
                preferred_element_type=F32)
    gate_ref[...] = jax.nn.silu(z).astype(BF16)

    out_refs = (qkv0_ref, qkv1_ref, qkv2_ref)
    low = lax.broadcasted_iota(jnp.int32, (tm, LANES), 1) < HEAD_DIM
    for g in range(N_GROUPS):
        r = DILATED_GROUPS[g][1]
        sub = tm // r
        for part in range(3):
            col0 = (g * 3 + part) * ATTN_WIDTH
            a = jnp.dot(h_ref[g], w_ref[:, col0:col0 + ATTN_WIDTH],
                        preferred_element_type=F32)
            gain_ref = (qg_ref, kg_ref, None)[part]
            for s in range(LANE_TILES):
                sl = slice(s * LANES, (s + 1) * LANES)
                a_s = a[:, sl]
                if gain_ref is not None:
                    sq = a_s * a_s
                    ss_low = jnp.sum(jnp.where(low, sq, 0.0), axis=-1, keepdims=True)
                    ss_high = jnp.sum(jnp.where(low, 0.0, sq), axis=-1, keepdims=True)
                    ss = jnp.where(low, ss_low, ss_high)
                    a_s = a_s * lax.rsqrt(ss * (1.0 / HEAD_DIM) + EPS) * gain_ref[g:g + 1, sl]
                a_s = a_s.astype(BF16)
                for c in range(r):
                    out_refs[g][c, :, part * ATTN_WIDTH + s * LANES:
                                part * ATTN_WIDTH + (s + 1) * LANES] = a_s[c * sub:(c + 1) * sub]


def _dilated_block(r, sub, width, tiles_per_seq):
    return pl.BlockSpec((None, r, sub, width),
                        lambda i: (i // tiles_per_seq, 0, i % tiles_per_seq, 0))


def _attn_proj(x2, norm_g, w_in, q_gain, k_gain, layer, batch, seq, next_weights):
    t = x2.shape[0]
    tm = TOKEN_TILE
    tiles_per_seq = seq // tm
    qg = jnp.tile(q_gain, (1, 1, N_HEADS)) * (HEAD_DIM ** -0.5 * LOG2E)
    kg = jnp.tile(k_gain, (1, 1, N_HEADS))
    qkv_specs, qkv_shapes = [], []
    for _, r in DILATED_GROUPS:
        qkv_specs.append(_dilated_block(r, tm // r, 3 * ATTN_WIDTH, tiles_per_seq))
        qkv_shapes.append(jax.ShapeDtypeStruct((batch, r, seq // r, 3 * ATTN_WIDTH), BF16))
    r_in, r_out, r_shapes = _cast_rider_specs(next_weights, layer + 1, t // tm)
    return pl.pallas_call(
        functools.partial(_attn_proj_kernel, tm=tm, n_riders=len(next_weights)),
        grid=(t // tm,),
        in_specs=[
            pl.BlockSpec((tm, D_MODEL), lambda i: (i, 0)),
            _resident((1, D_MODEL), layer),
            _resident((D_MODEL, QKV_COLS + ATTN_WIDTH)),
            _resident((N_GROUPS, ATTN_WIDTH), layer),
            _resident((N_GROUPS, ATTN_WIDTH), layer),
        ] + r_in,
        out_specs=qkv_specs + [pl.BlockSpec((tm, ATTN_WIDTH), lambda i: (i, 0))] + r_out,
        out_shape=qkv_shapes + [jax.ShapeDtypeStruct((t, ATTN_WIDTH), BF16)] + r_shapes,
        scratch_shapes=[pltpu.VMEM((N_GROUPS, tm, D_MODEL), BF16),
                        pltpu.VMEM((LANE_TILES, tm, LANES), F32)],
        compiler_params=_params(),
        name="attn_proj",
    )(x2, norm_g, w_in, qg, kg, *next_weights)


def _attn_group_kernel(q_ref, k_ref, v_ref, bias_ref, o_ref, stat_ref, *scratch,
                       ncls, nq, carry):
    if carry:
        kprev_ref, vprev_ref = scratch
        n = pl.program_id(2)

        @pl.when(n == 0)
        def _():
            kprev_ref[...] = jnp.zeros_like(kprev_ref)
            vprev_ref[...] = jnp.zeros_like(vprev_ref)

        first = jnp.where(n == 0, 1, 0)
    else:
        first = 1
    lane = lax.broadcasted_iota(jnp.int32, (BLOCK, LANES), 1)
    low = lane < HEAD_DIM
    lane_row = lax.broadcasted_iota(jnp.int32, (1, LANES), 1)
    low_b = (lane_row < HEAD_DIM).astype(F32).astype(BF16)
    high_b = (lane_row >= HEAD_DIM).astype(F32).astype(BF16)
    ones = jnp.ones((2 * BLOCK, LANES), BF16)
    no_prev = jnp.zeros((BLOCK, LANES), BF16)

    for ci in range(ncls):
        for qb in range(nq):
            rows = slice(qb * BLOCK, (qb + 1) * BLOCK)
            stat_tile = jnp.zeros((BLOCK, LANES), F32)
            for pr in range(N_PAIRS):
                cols = slice(pr * LANES, (pr + 1) * LANES)
                q = q_ref[ci, rows, cols]
                q2 = jnp.concatenate([q * low_b, q * high_b], axis=0)
                if qb == 0:
                    kp = kprev_ref[:, cols] if carry else no_prev
                    vp = vprev_ref[:, cols] if carry else no_prev
                    kcat = jnp.concatenate([kp, k_ref[ci, 0:BLOCK, cols]], axis=0)
                    vcat = jnp.concatenate([vp, v_ref[ci, 0:BLOCK, cols]], axis=0)
                    bidx = first
                else:
                    krows = slice((qb - 1) * BLOCK, (qb + 1) * BLOCK)
                    kcat = k_ref[ci, krows, cols]
                    vcat = v_ref[ci, krows, cols]
                    bidx = 0
                s = lax.dot_general(q2, kcat, (((1,), (1,)), ((), ())),
                                    preferred_element_type=F32) + bias_ref[bidx, pr]
                m = jnp.max(s, axis=1, keepdims=True)
                p = jnp.exp2(s - m).astype(BF16)
                oe = jnp.dot(p, jnp.concatenate([vcat, ones], axis=1),
                             preferred_element_type=F32)
                o = jnp.where(low, oe[0:BLOCK, :LANES], oe[BLOCK:2 * BLOCK, :LANES])
                o_ref[ci, rows, cols] = o.astype(BF16)
                l = oe[:, LANES:]
                for hh in range(HEADS_PER_TILE):
                    head = HEADS_PER_TILE * pr + hh
                    hrows = slice(hh * BLOCK, (hh + 1) * BLOCK)
                    stat_tile = jnp.where(lane == head, m[hrows], stat_tile)
                    stat_tile = jnp.where(lane == N_HEADS + head, l[hrows], stat_tile)
            stat_ref[ci, rows, :] = stat_tile

    if carry:
        kprev_ref[...] = k_ref[0, (nq - 1) * BLOCK:nq * BLOCK, :]
        vprev_ref[...] = v_ref[0, (nq - 1) * BLOCK:nq * BLOCK, :]


def _group_bias(rel_bias, g, dilation, span):
    a = jnp.arange(BLOCK)[:, None]
    kk = jnp.arange(2 * BLOCK)[None, :]
    step = BLOCK + a - kk
    valid = (step >= 0) & (step <= span)
    dist = jnp.clip(step, 0, span) * dilation
    max_exact = N_BUCKETS // 2
    d = jnp.maximum(dist, 1).astype(F32)
    large = max_exact + (jnp.log(d / max_exact) / math.log(MAX_DISTANCE / max_exact)
                         * (N_BUCKETS - max_exact)).astype(jnp.int32)
    large = jnp.minimum(large, N_BUCKETS - 1)
    bucket = jnp.where(dist < max_exact, dist, large)
    tab = rel_bias[:, g * N_HEADS:(g + 1) * N_HEADS].astype(F32)
    onehot = (bucket[..., None] == jnp.arange(N_BUCKETS)).astype(F32)
    bias = jnp.einsum("akn,nh->hak", onehot, tab, precision=lax.Precision.HIGHEST) * LOG2E
    interior = jnp.where(valid[None], bias, MASKED)
    first = jnp.where((valid & (kk >= BLOCK))[None], bias, MASKED)
    both = jnp.stack([interior, first])
    return both.reshape(2, N_PAIRS, HEADS_PER_TILE * BLOCK, 2 * BLOCK)


def _attn_group(qkv, rel_bias, g):
    batch, dilation, sub_len, _ = qkv.shape
    span = DILATED_GROUPS[g][0] // dilation
    nq = min(ATTN_BLOCKS_PER_STEP, sub_len // BLOCK)
    rows = nq * BLOCK
    carry = rows < sub_len
    ncls = 1 if carry else min(dilation, ATTN_BLOCKS_PER_STEP // nq)
    bias = _group_bias(rel_bias, g, dilation, span)

    def spec(width, col):
        return pl.BlockSpec((None, ncls, rows, width), lambda b, c, n: (b, c, n, col))

    scratch = [pltpu.VMEM((BLOCK, ATTN_WIDTH), BF16)] * 2 if carry else []
    return pl.pallas_call(
        functools.partial(_attn_group_kernel, ncls=ncls, nq=nq, carry=carry),
        grid=(batch, dilation // ncls, sub_len // rows),
        in_specs=[spec(ATTN_WIDTH, 0), spec(ATTN_WIDTH, 1), spec(ATTN_WIDTH, 2),
                  _resident(bias.shape)],
        out_specs=[spec(ATTN_WIDTH, 0), spec(LANES, 0)],
        out_shape=[jax.ShapeDtypeStruct((batch, dilation, sub_len, ATTN_WIDTH), BF16),
                   jax.ShapeDtypeStruct((batch, dilation, sub_len, LANES), F32)],
        scratch_shapes=scratch,
        compiler_params=_params(3),
        name=f"attn_group{g}",
    )(qkv, qkv, qkv, bias)


def _merge_out_kernel(x_ref, o0_ref, o1_ref, o2_ref, s0_ref, s1_ref, s2_ref,
                      gate_ref, expand_ref, w_ref, out_ref,
                      snat_ref, onat_ref, stage_ref, y_ref, *, tm, chunk):
    o_refs = (o0_ref, o1_ref, o2_ref)
    s_refs = (s0_ref, s1_ref, s2_ref)
    assert DILATED_GROUPS[0][1] == 1
    head_lane = lax.broadcasted_iota(jnp.int32, (chunk, LANES), 1) < N_HEADS
    tiles_per_dot = MXU_COLS // LANES

    for ck in range(tm // chunk):
        row0 = ck * chunk
        nat = slice(row0, row0 + chunk)

        def to_natural(dst, piece, r):
            sub = chunk // r
            src = slice(row0 // r, row0 // r + sub)
            if r <= FAST_STRIDE:
                for c in range(r):
                    dst[pl.ds(row0 + c, sub, stride=r), :] = piece(c, src)
                return
            outer = r // FAST_STRIDE
            for c0 in range(FAST_STRIDE):
                for c1 in range(outer):
                    stage_ref[ck, c0, pl.ds(c1, sub, stride=outer), :] = (
                        piece(c1 * FAST_STRIDE + c0, src))
                dst[pl.ds(row0 + c0, sub * outer, stride=FAST_STRIDE), :] = stage_ref[ck, c0]

        for g in range(1, N_GROUPS):
            r = DILATED_GROUPS[g][1]
            to_natural(snat_ref.at[g - 1], lambda c, src: s_refs[g][c, src, :], r)
            for j in range(LANE_TILES):
                to_natural(onat_ref.at[g - 1, j],
                           lambda c, src: o_refs[g][c, src, j * LANES:(j + 1) * LANES].astype(F32),
                           r)

        def o_tile(g, j):
            if g == 0:
                return o0_ref[0, nat, j * LANES:(j + 1) * LANES].astype(F32)
            return onat_ref[g - 1, j, nat, :]

        stats = [s0_ref[0, nat, :]] + [snat_ref[g - 1, nat, :] for g in range(1, N_GROUPS)]
        m = jnp.maximum(jnp.maximum(stats[0], stats[1]), stats[2])
        es = [jnp.exp2(st - m) for st in stats]
        den = sum(e * pltpu.roll(st, LANES - N_HEADS, 1) for e, st in zip(es, stats))
        inv = 1.0 / jnp.where(head_lane, den, 1.0)
        weights = []
        for e in es:
            alpha = jnp.where(head_lane, e * inv, 0.0)
            hi = alpha.astype(BF16)
            lo = (alpha - hi.astype(F32)).astype(BF16)
            weights.append(jnp.concatenate([hi, lo], axis=1))
        for jj in range(LANE_TILES // tiles_per_dot):
            spread = [jnp.dot(w, expand_ref[:, jj * MXU_COLS:(jj + 1) * MXU_COLS],
                              preferred_element_type=F32) for w in weights]
            for jt in range(tiles_per_dot):
                j = jj * tiles_per_dot + jt
                cols = slice(j * LANES, (j + 1) * LANES)
                o = sum(spread[g][:, jt * LANES:(jt + 1) * LANES] * o_tile(g, j)
                        for g in range(N_GROUPS))
                y_ref[nat, cols] = (o * gate_ref[nat, cols].astype(F32)).astype(BF16)
        out_ref[nat, :] = x_ref[nat, :] + jnp.dot(y_ref[nat, :], w_ref[...],
                                                  preferred_element_type=F32)


def _merge_out(x2, outs, stats, gate, w_out, seq):
    t = x2.shape[0]
    tm = 2 * TOKEN_TILE
    tiles_per_seq = seq // tm
    lane_head = jnp.arange(2 * LANES) % LANES
    expand = (lane_head[:, None] == (jnp.arange(ATTN_WIDTH) // HEAD_DIM)[None, :]).astype(BF16)
    row = lambda w: pl.BlockSpec((tm, w), lambda i: (i, 0))
    o_specs = [_dilated_block(r, tm // r, ATTN_WIDTH, tiles_per_seq) for _, r in DILATED_GROUPS]
    l_specs = [_dilated_block(r, tm // r, LANES, tiles_per_seq) for _, r in DILATED_GROUPS]
    return pl.pallas_call(
        functools.partial(_merge_out_kernel, tm=tm, chunk=TOKEN_TILE),
        grid=(t // tm,),
        in_specs=[row(D_MODEL)] + o_specs + l_specs
                 + [row(ATTN_WIDTH), _resident(expand.shape),
                    _resident((ATTN_WIDTH, D_MODEL))],
        out_specs=row(D_MODEL),
        out_shape=jax.ShapeDtypeStruct((t, D_MODEL), F32),
        scratch_shapes=[pltpu.VMEM((N_GROUPS - 1, tm, LANES), F32),
                        pltpu.VMEM((N_GROUPS - 1, LANE_TILES, tm, LANES), F32),
                        pltpu.VMEM((tm // TOKEN_TILE, FAST_STRIDE, TOKEN_TILE // FAST_STRIDE, LANES), F32),
                        pltpu.VMEM((tm, ATTN_WIDTH), BF16)],
        compiler_params=_params(),
        name="merge_out",
    )(x2, *outs, *stats, gate, expand, w_out)


def _attn_layer(x2, norm_g, w_in, q_gain, k_gain, w_out, rel_bias, layer, batch, seq,
                next_weights):
    qkv0, qkv1, qkv2, gate, *cast = _attn_proj(x2, norm_g, w_in, q_gain, k_gain, layer,
                                               batch, seq, next_weights)
    outs, stats = [], []
    for g, qkv in enumerate((qkv0, qkv1, qkv2)):
        o, st = _attn_group(qkv, rel_bias, g)
        outs.append(o)
        stats.append(st)
    return _merge_out(x2, outs, stats, gate, w_out, seq), cast


def kernel(x, conv_norm, conv_w_in, conv_w, conv_w_out, attn_norm, attn_w_in,
           attn_q_gain, attn_k_gain, attn_w_out, rel_bias):
    batch, seq, d = x.shape
    assert d == D_MODEL and seq % TOKEN_TILE == 0
    assert all(TOKEN_TILE % (BF16_SUBLANES * r) == 0 and seq % (BLOCK * r) == 0
               for _, r in DILATED_GROUPS)
    depth = conv_norm.shape[0] + attn_norm.shape[0]
    conv_norm3, attn_norm3 = conv_norm[:, None, :], attn_norm[:, None, :]
    w_in, w_out = conv_w_in[0].astype(BF16), conv_w_out[0].astype(BF16)
    x2 = x.reshape(batch * seq, d)
    for i in range(depth):
        j = i // 2
        if i % 2 == 0:
            nxt = (attn_w_in, attn_w_out) if i + 1 < depth else ()
            x2, cast = _conv_layer(x2, conv_norm3, w_in, conv_w, w_out, j, seq, nxt)
        else:
            nxt = (conv_w_in, conv_w_out) if i + 1 < depth else ()
            x2, cast = _attn_layer(x2, attn_norm3, w_in, attn_q_gain, attn_k_gain, w_out,
                                   rel_bias, j, batch, seq, nxt)
        if cast:
            w_in, w_out = cast
    return x2.reshape(batch, seq, d)
```

```python
import functools
import math

import jax
import jax.numpy as jnp
from jax import lax
from jax.experimental import pallas as pl
from jax.experimental.pallas import tpu as pltpu

D_MODEL = 1024
CONV_WIDTH = 2048
CONV_K = 3
HEAD_DIM = 64
N_HEADS = 16
ATTN_WIDTH = N_HEADS * HEAD_DIM
DILATED_GROUPS = ((128, 1), (512, 4), (2048, 16))
N_GROUPS = len(DILATED_GROUPS)
QKV_COLS = N_GROUPS * 3 * ATTN_WIDTH
BLOCK = 128
N_BUCKETS = 32
MAX_DISTANCE = 2048
EPS = 1e-6
MASKED = -1e30

LANES = 128
F32_SUBLANES = 8
BF16_SUBLANES = 16
MXU_COLS = 256
HEADS_PER_TILE = LANES // HEAD_DIM
N_PAIRS = N_HEADS // HEADS_PER_TILE
LANE_TILES = ATTN_WIDTH // LANES
VMEM_LIMIT = 56 * 1024 * 1024
TOKEN_TILE = 512
MERGE_CHUNK = 512
ATTN_BLOCKS_PER_STEP = 16
FAST_STRIDE = 4
LOG2E = math.log2(math.e)

F32 = jnp.float32
BF16 = jnp.bfloat16


def _resident(shape, layer=None):
    if layer is None:
        return pl.BlockSpec(shape, lambda *_: (0,) * len(shape),
                            pipeline_mode=pl.Buffered(1))
    return pl.BlockSpec((None,) + tuple(shape), lambda *_: (layer,) + (0,) * len(shape),
                        pipeline_mode=pl.Buffered(1))


def _cast_rider_specs(weights, layer, n_steps):
    in_specs, out_specs, out_shapes = [], [], []
    for w in weights:
        _, rows, cols = w.shape
        slab = rows // n_steps
        assert slab * n_steps == rows and slab % BF16_SUBLANES == 0
        in_specs.append(pl.BlockSpec((None, slab, cols), lambda i: (layer, i, 0)))
        out_specs.append(pl.BlockSpec((slab, cols), lambda i: (i, 0)))
        out_shapes.append(jax.ShapeDtypeStruct((rows, cols), BF16))
    return in_specs, out_specs, out_shapes


def _params(n_axes=1):
    return pltpu.CompilerParams(
        dimension_semantics=("arbitrary",) * n_axes, vmem_limit_bytes=VMEM_LIMIT)


def _rms_rows(x, gain):
    ms = jnp.mean(x * x, axis=-1, keepdims=True)
    return x * lax.rsqrt(ms + EPS) * gain


def _conv_layer_kernel(*refs, tm, te, tiles_per_seq, n_riders):
    x_ref, g_ref, win_ref, wc_ref, wout_ref = refs[:5]
    rider_in = refs[5:5 + n_riders]
    o_ref = refs[5 + n_riders]
    rider_out = refs[6 + n_riders:6 + 2 * n_riders]
    h_ref, carry_ref = refs[6 + 2 * n_riders:]
    for src, dst in zip(rider_in, rider_out):
        dst[...] = src[...].astype(BF16)
    i = pl.program_id(0)
    x = x_ref[...]
    h_ref[...] = _rms_rows(x, g_ref[...]).astype(BF16)

    @pl.when(i % tiles_per_seq == 0)
    def _():
        carry_ref[...] = jnp.zeros_like(carry_ref)

    rows = lax.broadcasted_iota(jnp.int32, (tm, te), 0)
    for j in range(CONV_WIDTH // te):
        cols = slice(j * te, (j + 1) * te)

        def proj(p):
            w = win_ref[:, p * CONV_WIDTH + j * te:p * CONV_WIDTH + (j + 1) * te]
            return jnp.dot(h_ref[...], w, preferred_element_type=F32)

        v = proj(1) * proj(2)
        prev1 = carry_ref[1:2, cols]
        prev2 = carry_ref[0:1, cols]
        v1 = jnp.where(rows == 0, prev1, pltpu.roll(v, 1, 0))
        v2 = jnp.where(rows == 0, prev2,
                       jnp.where(rows == 1, prev1, pltpu.roll(v, 2, 0)))
        carry_ref[0:2, cols] = v[tm - 2:tm]
        wc = wc_ref[:, cols]
        conv = wc[0:1] * v2 + wc[1:2] * v1 + wc[2:3] * v
        y = proj(0) * conv * jax.nn.silu(proj(3))
        part = jnp.dot(y.astype(BF16), wout_ref[cols, :],
                       preferred_element_type=F32)
        if j == 0:
            o_ref[...] = x + part
        else:
            o_ref[...] += part


def _conv_layer(x2, norm_g, w_in, w_conv, w_out, layer, seq, next_weights):
    t = x2.shape[0]
    tm, te = 2 * TOKEN_TILE, MXU_COLS
    r_in, r_out, r_shapes = _cast_rider_specs(next_weights, layer, t // tm)
    kern = functools.partial(_conv_layer_kernel, tm=tm, te=te,
                             tiles_per_seq=seq // tm, n_riders=len(next_weights))
    out, *cast = pl.pallas_call(
        kern,
        grid=(t // tm,),
        in_specs=[
            pl.BlockSpec((tm, D_MODEL), lambda i: (i, 0)),
            _resident((1, D_MODEL), layer),
            _resident((D_MODEL, 4 * CONV_WIDTH)),
            _resident((CONV_K, CONV_WIDTH), layer),
            _resident((CONV_WIDTH, D_MODEL)),
        ] + r_in,
        out_specs=[pl.BlockSpec((tm, D_MODEL), lambda i: (i, 0))] + r_out,
        out_shape=[jax.ShapeDtypeStruct((t, D_MODEL), F32)] + r_shapes,
        scratch_shapes=[pltpu.VMEM((tm, D_MODEL), BF16),
                        pltpu.VMEM((F32_SUBLANES, CONV_WIDTH), F32)],
        compiler_params=_params(),
        name="conv_layer",
    )(x2, norm_g, w_in, w_conv, w_out, *next_weights)
    return out, cast


def _attn_proj_kernel(*refs, tm, n_riders):
    x_ref, g_ref, w_ref, qg_ref, kg_ref = refs[:5]
    rider_in = refs[5:5 + n_riders]
    qkv0_ref, qkv1_ref, qkv2_ref, gate_ref = refs[5 + n_riders:9 + n_riders]
    rider_out = refs[9 + n_riders:9 + 2 * n_riders]
    h_ref, slab_ref = refs[9 + 2 * n_riders:]
    for src, dst in zip(rider_in, rider_out):
        dst[...] = src[...].astype(BF16)
    h32 = _rms_rows(x_ref[...], g_ref[...])
    h_ref[0] = h32.astype(BF16)
    for j in range(LANE_TILES):
        slab_ref[j] = h32[:, j * LANES:(j + 1) * LANES]
    for g in range(1, N_GROUPS):
        r = DILATED_GROUPS[g][1]
        sub = tm // r
        for c in range(r):
            for j in range(LANE_TILES):
                piece = slab_ref[j, pl.ds(c, sub, stride=r), :]
                h_ref[g, c * sub:(c + 1) * sub, j * LANES:(j + 1) * LANES] = piece.astype(BF16)

    z = jnp.dot(h_ref[0], w_ref[:, QKV_COLS:QKV_COLS + ATTN_WIDTH],
                preferred_element_type=F32)
    gate_ref[...] = jax.nn.silu(z).astype(BF16)

    out_refs = (qkv0_ref, qkv1_ref, qkv2_ref)
    low = lax.broadcasted_iota(jnp.int32, (tm, LANES), 1) < HEAD_DIM
    for g in range(N_GROUPS):
        r = DILATED_GROUPS[g][1]
        sub = tm // r
        for part in range(3):
            col0 = (g * 3 + part) * ATTN_WIDTH
            a = jnp.dot(h_ref[g], w_ref[:, col0:col0 + ATTN_WIDTH],
                        preferred_element_type=F32)
            gain_ref = (qg_ref, kg_ref, None)[part]
            for s in range(LANE_TILES):
                sl = slice(s * LANES, (s + 1) * LANES)
                a_s = a[:, sl]
                if gain_ref is not None:
                    sq = a_s * a_s
                    ss_low = jnp.sum(jnp.where(low, sq, 0.0), axis=-1, keepdims=True)
                    ss_high = jnp.sum(jnp.where(low, 0.0, sq), axis=-1, keepdims=True)
                    ss = jnp.where(low, ss_low, ss_high)
                    a_s = a_s * lax.rsqrt(ss * (1.0 / HEAD_DIM) + EPS) * gain_ref[g:g + 1, sl]
                a_s = a_s.astype(BF16)
                for c in range(r):
                    out_refs[g][c, :, part * ATTN_WIDTH + s * LANES:
                                part * ATTN_WIDTH + (s + 1) * LANES] = a_s[c * sub:(c + 1) * sub]


def _dilated_block(r, sub, width, tiles_per_seq):
    return pl.BlockSpec((None, r, sub, width),
                        lambda i: (i // tiles_per_seq, 0, i % tiles_per_seq, 0))


def _attn_proj(x2, norm_g, w_in, q_gain, k_gain, layer, batch, seq, next_weights):
    t = x2.shape[0]
    tm = TOKEN_TILE
    tiles_per_seq = seq // tm
    qg = jnp.tile(q_gain, (1, 1, N_HEADS)) * (HEAD_DIM ** -0.5 * LOG2E)
    kg = jnp.tile(k_gain, (1, 1, N_HEADS))
    qkv_specs, qkv_shapes = [], []
    for _, r in DILATED_GROUPS:
        qkv_specs.append(_dilated_block(r, tm // r, 3 * ATTN_WIDTH, tiles_per_seq))
        qkv_shapes.append(jax.ShapeDtypeStruct((batch, r, seq // r, 3 * ATTN_WIDTH), BF16))
    r_in, r_out, r_shapes = _cast_rider_specs(next_weights, layer + 1, t // tm)
    return pl.pallas_call(
        functools.partial(_attn_proj_kernel, tm=tm, n_riders=len(next_weights)),
        grid=(t // tm,),
        in_specs=[
            pl.BlockSpec((tm, D_MODEL), lambda i: (i, 0)),
            _resident((1, D_MODEL), layer),
            _resident((D_MODEL, QKV_COLS + ATTN_WIDTH)),
            _resident((N_GROUPS, ATTN_WIDTH), layer),
            _resident((N_GROUPS, ATTN_WIDTH), layer),
        ] + r_in,
        out_specs=qkv_specs + [pl.BlockSpec((tm, ATTN_WIDTH), lambda i: (i, 0))] + r_out,
        out_shape=qkv_shapes + [jax.ShapeDtypeStruct((t, ATTN_WIDTH), BF16)] + r_shapes,
        scratch_shapes=[pltpu.VMEM((N_GROUPS, tm, D_MODEL), BF16),
                        pltpu.VMEM((LANE_TILES, tm, LANES), F32)],
        compiler_params=_params(),
        name="attn_proj",
    )(x2, norm_g, w_in, qg, kg, *next_weights)


def _attend_classes(q_ref, k_ref, v_ref, bias_ref, o_ref, stat_ref, kprev_ref, vprev_ref,
                    first, *, ncls, nq):
    lane = lax.broadcasted_iota(jnp.int32, (BLOCK, LANES), 1)
    low = lane < HEAD_DIM
    lane_row = lax.broadcasted_iota(jnp.int32, (1, LANES), 1)
    low_b = (lane_row < HEAD_DIM).astype(F32).astype(BF16)
    high_b = (lane_row >= HEAD_DIM).astype(F32).astype(BF16)
    ones = jnp.ones((2 * BLOCK, LANES), BF16)
    no_prev = jnp.zeros((BLOCK, LANES), BF16)

    for ci in range(ncls):
        for qb in range(nq):
            rows = slice(qb * BLOCK, (qb + 1) * BLOCK)
            stat_tile = jnp.zeros((BLOCK, LANES), F32)
            for pr in range(N_PAIRS):
                cols = slice(pr * LANES, (pr + 1) * LANES)
                q = q_ref[ci, rows, cols]
                q2 = jnp.concatenate([q * low_b, q * high_b], axis=0)
                if qb == 0:
                    kp = no_prev if kprev_ref is None else kprev_ref[:, cols]
                    vp = no_prev if vprev_ref is None else vprev_ref[:, cols]
                    kcat = jnp.concatenate([kp, k_ref[ci, 0:BLOCK, cols]], axis=0)
                    vcat = jnp.concatenate([vp, v_ref[ci, 0:BLOCK, cols]], axis=0)
                    bidx = first
                else:
                    krows = slice((qb - 1) * BLOCK, (qb + 1) * BLOCK)
                    kcat = k_ref[ci, krows, cols]
                    vcat = v_ref[ci, krows, cols]
                    bidx = 0
                s = lax.dot_general(q2, kcat, (((1,), (1,)), ((), ())),
                                    preferred_element_type=F32) + bias_ref[bidx, pr]
                m = jnp.max(s, axis=1, keepdims=True)
                p = jnp.exp2(s - m).astype(BF16)
                oe = jnp.dot(p, jnp.concatenate([vcat, ones], axis=1),
                             preferred_element_type=F32)
                o = jnp.where(low, oe[0:BLOCK, :LANES], oe[BLOCK:2 * BLOCK, :LANES])
                o_ref[ci, rows, cols] = o.astype(BF16)
                l = oe[:, LANES:]
                for hh in range(HEADS_PER_TILE):
                    head = HEADS_PER_TILE * pr + hh
                    hrows = slice(hh * BLOCK, (hh + 1) * BLOCK)
                    stat_tile = jnp.where(lane == head, m[hrows], stat_tile)
                    stat_tile = jnp.where(lane == N_HEADS + head, l[hrows], stat_tile)
            stat_ref[ci, rows, :] = stat_tile


def _attn_group_kernel(q_ref, k_ref, v_ref, bias_ref, o_ref, stat_ref, *scratch,
                       ncls, nq, carry):
    if not carry:
        _attend_classes(q_ref, k_ref, v_ref, bias_ref, o_ref, stat_ref, None, None, 1,
                        ncls=ncls, nq=nq)
        return
    kprev_ref, vprev_ref = scratch
    n = pl.program_id(2)

    @pl.when(n == 0)
    def _():
        kprev_ref[...] = jnp.zeros_like(kprev_ref)
        vprev_ref[...] = jnp.zeros_like(vprev_ref)

    _attend_classes(q_ref, k_ref, v_ref, bias_ref, o_ref, stat_ref, kprev_ref, vprev_ref,
                    jnp.where(n == 0, 1, 0), ncls=ncls, nq=nq)
    kprev_ref[...] = k_ref[0, (nq - 1) * BLOCK:nq * BLOCK, :]
    vprev_ref[...] = v_ref[0, (nq - 1) * BLOCK:nq * BLOCK, :]


def _group_bias(rel_bias, g):
    window, dilation = DILATED_GROUPS[g]
    span = window // dilation
    a = jnp.arange(BLOCK)[:, None]
    kk = jnp.arange(2 * BLOCK)[None, :]
    step = BLOCK + a - kk
    valid = (step >= 0) & (step <= span)
    dist = jnp.clip(step, 0, span) * dilation
    max_exact = N_BUCKETS // 2
    d = jnp.maximum(dist, 1).astype(F32)
    large = max_exact + (jnp.log(d / max_exact) / math.log(MAX_DISTANCE / max_exact)
                         * (N_BUCKETS - max_exact)).astype(jnp.int32)
    large = jnp.minimum(large, N_BUCKETS - 1)
    bucket = jnp.where(dist < max_exact, dist, large)
    tab = rel_bias[:, g * N_HEADS:(g + 1) * N_HEADS].astype(F32)
    onehot = (bucket[..., None] == jnp.arange(N_BUCKETS)).astype(F32)
    bias = jnp.einsum("akn,nh->hak", onehot, tab, precision=lax.Precision.HIGHEST) * LOG2E
    interior = jnp.where(valid[None], bias, MASKED)
    first = jnp.where((valid & (kk >= BLOCK))[None], bias, MASKED)
    both = jnp.stack([interior, first])
    return both.reshape(2, N_PAIRS, HEADS_PER_TILE * BLOCK, 2 * BLOCK)


def _attn_group(qkv, bias, g):
    batch, dilation, sub_len, _ = qkv.shape
    nq = min(ATTN_BLOCKS_PER_STEP, sub_len // BLOCK)
    rows = nq * BLOCK
    carry = rows < sub_len
    ncls = 1 if carry else min(dilation, ATTN_BLOCKS_PER_STEP // nq)

    def spec(width, col):
        return pl.BlockSpec((None, ncls, rows, width), lambda b, c, n: (b, c, n, col))

    scratch = [pltpu.VMEM((BLOCK, ATTN_WIDTH), BF16)] * 2 if carry else []
    return pl.pallas_call(
        functools.partial(_attn_group_kernel, ncls=ncls, nq=nq, carry=carry),
        grid=(batch, dilation // ncls, sub_len // rows),
        in_specs=[spec(ATTN_WIDTH, 0), spec(ATTN_WIDTH, 1), spec(ATTN_WIDTH, 2),
                  _resident(bias.shape)],
        out_specs=[spec(ATTN_WIDTH, 0), spec(LANES, 0)],
        out_shape=[jax.ShapeDtypeStruct((batch, dilation, sub_len, ATTN_WIDTH), BF16),
                   jax.ShapeDtypeStruct((batch, dilation, sub_len, LANES), F32)],
        scratch_shapes=scratch,
        compiler_params=_params(3),
        name=f"attn_group{g}",
    )(qkv, qkv, qkv, bias)


def _merge_out_kernel(x_ref, q_ref, k_ref, v_ref, bias_ref, o1_ref, o2_ref, s1_ref, s2_ref,
                      gate_ref, expand_ref, w_ref, out_ref,
                      kprev_ref, vprev_ref, o0_ref, s0_ref,
                      snat_ref, onat_ref, stage_ref, y_ref, *, tm, chunk, tiles_per_seq):
    o_refs = (o0_ref, o1_ref, o2_ref)
    s_refs = (s0_ref, s1_ref, s2_ref)
    assert DILATED_GROUPS[0][1] == 1

    seq_start = pl.program_id(0) % tiles_per_seq == 0

    @pl.when(seq_start)
    def _():
        kprev_ref[...] = jnp.zeros_like(kprev_ref)
        vprev_ref[...] = jnp.zeros_like(vprev_ref)

    head_lane = lax.broadcasted_iota(jnp.int32, (chunk, LANES), 1) < N_HEADS
    tiles_per_dot = MXU_COLS // LANES

    for ck in range(tm // chunk):
        row0 = ck * chunk
        nat = slice(row0, row0 + chunk)
        if ck == 0:
            before = (kprev_ref, vprev_ref, jnp.where(seq_start, 1, 0))
        else:
            before = (k_ref.at[0, row0 - BLOCK:row0, :], v_ref.at[0, row0 - BLOCK:row0, :], 0)
        _attend_classes(q_ref.at[:, nat, :], k_ref.at[:, nat, :], v_ref.at[:, nat, :], bias_ref,
                        o0_ref.at[:, nat, :], s0_ref.at[:, nat, :], *before,
                        ncls=1, nq=chunk // BLOCK)

        def to_natural(dst, piece, r):
            sub = chunk // r
            src = slice(row0 // r, row0 // r + sub)
            if r <= FAST_STRIDE:
                for c in range(r):
                    dst[pl.ds(row0 + c, sub, stride=r), :] = piece(c, src)
                return
            outer = r // FAST_STRIDE
            for c0 in range(FAST_STRIDE):
                for c1 in range(outer):
                    stage_ref[ck, c0, pl.ds(c1, sub, stride=outer), :] = (
                        piece(c1 * FAST_STRIDE + c0, src))
                dst[pl.ds(row0 + c0, sub * outer, stride=FAST_STRIDE), :] = stage_ref[ck, c0]

        for g in range(1, N_GROUPS):
            r = DILATED_GROUPS[g][1]
            to_natural(snat_ref.at[g - 1], lambda c, src: s_refs[g][c, src, :], r)
            for j in range(LANE_TILES):
                to_natural(onat_ref.at[g - 1, j],
                           lambda c, src: o_refs[g][c, src, j * LANES:(j + 1) * LANES].astype(F32),
                           r)

        def o_tile(g, j):
            if g == 0:
                return o0_ref[0, nat, j * LANES:(j + 1) * LANES].astype(F32)
            return onat_ref[g - 1, j, nat, :]

        stats = [s0_ref[0, nat, :]] + [snat_ref[g - 1, nat, :] for g in range(1, N_GROUPS)]
        m = jnp.maximum(jnp.maximum(stats[0], stats[1]), stats[2])
        es = [jnp.exp2(st - m) for st in stats]
        den = sum(e * pltpu.roll(st, LANES - N_HEADS, 1) for e, st in zip(es, stats))
        inv = 1.0 / jnp.where(head_lane, den, 1.0)
        weights = []
        for e in es:
            alpha = jnp.where(head_lane, e * inv, 0.0)
            hi = alpha.astype(BF16)
            lo = (alpha - hi.astype(F32)).astype(BF16)
            weights.append(jnp.concatenate([hi, lo], axis=1))
        for jj in range(LANE_TILES // tiles_per_dot):
            spread = [jnp.dot(w, expand_ref[:, jj * MXU_COLS:(jj + 1) * MXU_COLS],
                              preferred_element_type=F32) for w in weights]
            for jt in range(tiles_per_dot):
                j = jj * tiles_per_dot + jt
                cols = slice(j * LANES, (j + 1) * LANES)
                o = sum(spread[g][:, jt * LANES:(jt + 1) * LANES] * o_tile(g, j)
                        for g in range(N_GROUPS))
                y_ref[nat, cols] = (o * gate_ref[nat, cols].astype(F32)).astype(BF16)
        out_ref[nat, :] = x_ref[nat, :] + jnp.dot(y_ref[nat, :], w_ref[...],
                                                  preferred_element_type=F32)

    kprev_ref[...] = k_ref[0, tm - BLOCK:tm, :]
    vprev_ref[...] = v_ref[0, tm - BLOCK:tm, :]


def _merge_out(x2, qkv0, bias0, outs, stats, gate, w_out, seq):
    t = x2.shape[0]
    tm, chunk = TOKEN_TILE, MERGE_CHUNK
    tiles_per_seq = seq // tm
    lane_head = jnp.arange(2 * LANES) % LANES
    expand = (lane_head[:, None] == (jnp.arange(ATTN_WIDTH) // HEAD_DIM)[None, :]).astype(BF16)
    row = lambda w: pl.BlockSpec((tm, w), lambda i: (i, 0))
    qkv_spec = lambda part: pl.BlockSpec(
        (None, 1, tm, ATTN_WIDTH), lambda i: (i // tiles_per_seq, 0, i % tiles_per_seq, part))
    dilated = DILATED_GROUPS[1:]
    o_specs = [_dilated_block(r, tm // r, ATTN_WIDTH, tiles_per_seq) for _, r in dilated]
    l_specs = [_dilated_block(r, tm // r, LANES, tiles_per_seq) for _, r in dilated]
    return pl.pallas_call(
        functools.partial(_merge_out_kernel, tm=tm, chunk=chunk, tiles_per_seq=tiles_per_seq),
        grid=(t // tm,),
        in_specs=[row(D_MODEL), qkv_spec(0), qkv_spec(1), qkv_spec(2), _resident(bias0.shape)]
                 + o_specs + l_specs
                 + [row(ATTN_WIDTH), _resident(expand.shape),
                    _resident((ATTN_WIDTH, D_MODEL))],
        out_specs=row(D_MODEL),
        out_shape=jax.ShapeDtypeStruct((t, D_MODEL), F32),
        scratch_shapes=[pltpu.VMEM((BLOCK, ATTN_WIDTH), BF16),
                        pltpu.VMEM((BLOCK, ATTN_WIDTH), BF16),
                        pltpu.VMEM((1, tm, ATTN_WIDTH), BF16),
                        pltpu.VMEM((1, tm, LANES), F32),
                        pltpu.VMEM((N_GROUPS - 1, tm, LANES), F32),
                        pltpu.VMEM((N_GROUPS - 1, LANE_TILES, tm, LANES), F32),
                        pltpu.VMEM((tm // chunk, FAST_STRIDE, chunk // FAST_STRIDE, LANES), F32),
                        pltpu.VMEM((tm, ATTN_WIDTH), BF16)],
        compiler_params=_params(),
        name="merge_out",
    )(x2, qkv0, qkv0, qkv0, bias0, *outs, *stats, gate, expand, w_out)


def _attn_layer(x2, norm_g, w_in, q_gain, k_gain, w_out, rel_bias, layer, batch, seq,
                next_weights):
    qkv0, qkv1, qkv2, gate, *cast = _attn_proj(x2, norm_g, w_in, q_gain, k_gain, layer,
                                               batch, seq, next_weights)
    outs, stats = [], []
    for g, qkv in ((1, qkv1), (2, qkv2)):
        o, st = _attn_group(qkv, _group_bias(rel_bias, g), g)
        outs.append(o)
        stats.append(st)
    out = _merge_out(x2, qkv0, _group_bias(rel_bias, 0), outs, stats, gate, w_out, seq)
    return out, cast


def kernel(x, conv_norm, conv_w_in, conv_w, conv_w_out, attn_norm, attn_w_in,
           attn_q_gain, attn_k_gain, attn_w_out, rel_bias):
    batch, seq, d = x.shape
    assert d == D_MODEL and seq % TOKEN_TILE == 0
    assert all(TOKEN_TILE % (BF16_SUBLANES * r) == 0 and seq % (BLOCK * r) == 0
               for _, r in DILATED_GROUPS)
    depth = conv_norm.shape[0] + attn_norm.shape[0]
    conv_norm3, attn_norm3 = conv_norm[:, None, :], attn_norm[:, None, :]
    w_in, w_out = conv_w_in[0].astype(BF16), conv_w_out[0].astype(BF16)
    x2 = x.reshape(batch * seq, d)
    for i in range(depth):
        j = i // 2
        if i % 2 == 0:
            nxt = (attn_w_in, attn_w_out) if i + 1 < depth else ()
            x2, cast = _conv_layer(x2, conv_norm3, w_in, conv_w, w_out, j, seq, nxt)
        else:
            nxt = (conv_w_in, conv_w_out) if i + 1 < depth else ()
            x2, cast = _attn_layer(x2, attn_norm3, w_in, attn_q_gain, attn_k_gain, w_out,
                                   rel_bias, j, batch, seq, nxt)
        if cast:
            w_in, w_out = cast
    return x2.reshape(batch, seq, d)
```

```python
import functools
import math

import jax
import jax.numpy as jnp
from jax import lax
from jax.experimental import pallas as pl
from jax.experimental.pallas import tpu as pltpu

D_MODEL = 1024
CONV_WIDTH = 2048
CONV_K = 3
HEAD_DIM = 64
N_HEADS = 16
ATTN_WIDTH = N_HEADS * HEAD_DIM
DILATED_GROUPS = ((128, 1), (512, 4), (2048, 16))
N_GROUPS = len(DILATED_GROUPS)
QKV_COLS = N_GROUPS * 3 * ATTN_WIDTH
BLOCK = 128
N_BUCKETS = 32
MAX_DISTANCE = 2048
EPS = 1e-6
MASKED = -1e30

LANES = 128
F32_SUBLANES = 8
BF16_SUBLANES = 16
MXU_COLS = 256
HEADS_PER_TILE = LANES // HEAD_DIM
N_PAIRS = N_HEADS // HEADS_PER_TILE
LANE_TILES = ATTN_WIDTH // LANES
VMEM_LIMIT = 56 * 1024 * 1024
TOKEN_TILE = 512
ATTN_BLOCKS_PER_STEP = 16
FAST_STRIDE = 4
LOG2E = math.log2(math.e)

F32 = jnp.float32
BF16 = jnp.bfloat16


def _resident(shape, layer=None):
    if layer is None:
        return pl.BlockSpec(shape, lambda *_: (0,) * len(shape),
                            pipeline_mode=pl.Buffered(1))
    return pl.BlockSpec((None,) + tuple(shape), lambda *_: (layer,) + (0,) * len(shape),
                        pipeline_mode=pl.Buffered(1))


def _cast_rider_specs(weights, layer, n_steps):
    in_specs, out_specs, out_shapes = [], [], []
    for w in weights:
        _, rows, cols = w.shape
        slab = rows // n_steps
        assert slab * n_steps == rows and slab % BF16_SUBLANES == 0
        in_specs.append(pl.BlockSpec((None, slab, cols), lambda i: (layer, i, 0)))
        out_specs.append(pl.BlockSpec((slab, cols), lambda i: (i, 0)))
        out_shapes.append(jax.ShapeDtypeStruct((rows, cols), BF16))
    return in_specs, out_specs, out_shapes


def _params(n_axes=1):
    return pltpu.CompilerParams(
        dimension_semantics=("arbitrary",) * n_axes, vmem_limit_bytes=VMEM_LIMIT)


def _rms_rows(x, gain):
    ms = jnp.mean(x * x, axis=-1, keepdims=True)
    return x * lax.rsqrt(ms + EPS) * gain


def _conv_layer_kernel(*refs, tm, te, tiles_per_seq, n_riders):
    x_ref, g_ref, win_ref, wc_ref, wout_ref = refs[:5]
    rider_in = refs[5:5 + n_riders]
    o_ref = refs[5 + n_riders]
    rider_out = refs[6 + n_riders:6 + 2 * n_riders]
    h_ref, carry_ref = refs[6 + 2 * n_riders:]
    for src, dst in zip(rider_in, rider_out):
        dst[...] = src[...].astype(BF16)
    i = pl.program_id(0)
    x = x_ref[...]
    h_ref[...] = _rms_rows(x, g_ref[...]).astype(BF16)

    @pl.when(i % tiles_per_seq == 0)
    def _():
        carry_ref[...] = jnp.zeros_like(carry_ref)

    rows = lax.broadcasted_iota(jnp.int32, (tm, te), 0)
    for j in range(CONV_WIDTH // te):
        cols = slice(j * te, (j + 1) * te)

        def proj(p):
            w = win_ref[:, p * CONV_WIDTH + j * te:p * CONV_WIDTH + (j + 1) * te]
            return jnp.dot(h_ref[...], w, preferred_element_type=F32)

        v = proj(1) * proj(2)
        prev1 = carry_ref[1:2, cols]
        prev2 = carry_ref[0:1, cols]
        v1 = jnp.where(rows == 0, prev1, pltpu.roll(v, 1, 0))
        v2 = jnp.where(rows == 0, prev2,
                       jnp.where(rows == 1, prev1, pltpu.roll(v, 2, 0)))
        carry_ref[0:2, cols] = v[tm - 2:tm]
        wc = wc_ref[:, cols]
        conv = wc[0:1] * v2 + wc[1:2] * v1 + wc[2:3] * v
        y = proj(0) * conv * jax.nn.silu(proj(3))
        part = jnp.dot(y.astype(BF16), wout_ref[cols, :],
                       preferred_element_type=F32)
        if j == 0:
            o_ref[...] = x + part
        else:
            o_ref[...] += part


def _conv_layer(x2, norm_g, w_in, w_conv, w_out, layer, seq, next_weights):
    t = x2.shape[0]
    tm, te = 2 * TOKEN_TILE, MXU_COLS
    r_in, r_out, r_shapes = _cast_rider_specs(next_weights, layer, t // tm)
    kern = functools.partial(_conv_layer_kernel, tm=tm, te=te,
                             tiles_per_seq=seq // tm, n_riders=len(next_weights))
    out, *cast = pl.pallas_call(
        kern,
        grid=(t // tm,),
        in_specs=[
            pl.BlockSpec((tm, D_MODEL), lambda i: (i, 0)),
            _resident((1, D_MODEL), layer),
            _resident((D_MODEL, 4 * CONV_WIDTH)),
            _resident((CONV_K, CONV_WIDTH), layer),
            _resident((CONV_WIDTH, D_MODEL)),
        ] + r_in,
        out_specs=[pl.BlockSpec((tm, D_MODEL), lambda i: (i, 0))] + r_out,
        out_shape=[jax.ShapeDtypeStruct((t, D_MODEL), F32)] + r_shapes,
        scratch_shapes=[pltpu.VMEM((tm, D_MODEL), BF16),
                        pltpu.VMEM((F32_SUBLANES, CONV_WIDTH), F32)],
        compiler_params=_params(),
        name="conv_layer",
    )(x2, norm_g, w_in, w_conv, w_out, *next_weights)
    return out, cast


def _attn_proj_kernel(*refs, tm, n_riders):
    x_ref, g_ref, w_ref, qg_ref, kg_ref = refs[:5]
    rider_in = refs[5:5 + n_riders]
    qkv0_ref, qkv1_ref, qkv2_ref, gate_ref = refs[5 + n_riders:9 + n_riders]
    rider_out = refs[9 + n_riders:9 + 2 * n_riders]
    h_ref, slab_ref = refs[9 + 2 * n_riders:]
    for src, dst in zip(rider_in, rider_out):
        dst[...] = src[...].astype(BF16)
    h32 = _rms_rows(x_ref[...], g_ref[...])
    h_ref[0] = h32.astype(BF16)
    for j in range(LANE_TILES):
        slab_ref[j] = h32[:, j * LANES:(j + 1) * LANES]
    for g in range(1, N_GROUPS):
        r = DILATED_GROUPS[g][1]
        sub = tm // r
        for c in range(r):
            for j in range(LANE_TILES):
                piece = slab_ref[j, pl.ds(c, sub, stride=r), :]
                h_ref[g, c * sub:(c + 1) * sub, j * LANES:(j + 1) * LANES] = piece.astype(BF16)

    z = jnp.dot(h_ref[0], w_ref[:, QKV_COLS:QKV_COLS + ATTN_WIDTH],
                preferred_element_type=F32)
    gate_ref[...] = jax.nn.silu(z).astype(BF16)

    out_refs = (qkv0_ref, qkv1_ref, qkv2_ref)
    low = lax.broadcasted_iota(jnp.int32, (tm, LANES), 1) < HEAD_DIM
    for g in range(N_GROUPS):
        r = DILATED_GROUPS[g][1]
        sub = tm // r
        for part in range(3):
            col0 = (g * 3 + part) * ATTN_WIDTH
            a = jnp.dot(h_ref[g], w_ref[:, col0:col0 + ATTN_WIDTH],
                        preferred_element_type=F32)
            gain_ref = (qg_ref, kg_ref, None)[part]
            for s in range(LANE_TILES):
                sl = slice(s * LANES, (s + 1) * LANES)
                a_s = a[:, sl]
                if gain_ref is not None:
                    sq = a_s * a_s
                    ss_low = jnp.sum(jnp.where(low, sq, 0.0), axis=-1, keepdims=True)
                    ss_high = jnp.sum(jnp.where(low, 0.0, sq), axis=-1, keepdims=True)
                    ss = jnp.where(low, ss_low, ss_high)
                    a_s = a_s * lax.rsqrt(ss * (1.0 / HEAD_DIM) + EPS) * gain_ref[g:g + 1, sl]
                a_s = a_s.astype(BF16)
                for c in range(r):
                    out_refs[g][c, :, part * ATTN_WIDTH + s * LANES:
                                part * ATTN_WIDTH + (s + 1) * LANES] = a_s[c * sub:(c + 1) * sub]


def _dilated_block(r, sub, width, tiles_per_seq):
    return pl.BlockSpec((None, r, sub, width),
                        lambda i: (i // tiles_per_seq, 0, i % tiles_per_seq, 0))


def _attn_proj(x2, norm_g, w_in, q_gain, k_gain, layer, batch, seq, next_weights):
    t = x2.shape[0]
    tm = TOKEN_TILE
    tiles_per_seq = seq // tm
    qg = jnp.tile(q_gain, (1, 1, N_HEADS)) * (HEAD_DIM ** -0.5 * LOG2E)
    kg = jnp.tile(k_gain, (1, 1, N_HEADS))
    qkv_specs, qkv_shapes = [], []
    for _, r in DILATED_GROUPS:
        qkv_specs.append(_dilated_block(r, tm // r, 3 * ATTN_WIDTH, tiles_per_seq))
        qkv_shapes.append(jax.ShapeDtypeStruct((batch, r, seq // r, 3 * ATTN_WIDTH), BF16))
    r_in, r_out, r_shapes = _cast_rider_specs(next_weights, layer + 1, t // tm)
    return pl.pallas_call(
        functools.partial(_attn_proj_kernel, tm=tm, n_riders=len(next_weights)),
        grid=(t // tm,),
        in_specs=[
            pl.BlockSpec((tm, D_MODEL), lambda i: (i, 0)),
            _resident((1, D_MODEL), layer),
            _resident((D_MODEL, QKV_COLS + ATTN_WIDTH)),
            _resident((N_GROUPS, ATTN_WIDTH), layer),
            _resident((N_GROUPS, ATTN_WIDTH), layer),
        ] + r_in,
        out_specs=qkv_specs + [pl.BlockSpec((tm, ATTN_WIDTH), lambda i: (i, 0))] + r_out,
        out_shape=qkv_shapes + [jax.ShapeDtypeStruct((t, ATTN_WIDTH), BF16)] + r_shapes,
        scratch_shapes=[pltpu.VMEM((N_GROUPS, tm, D_MODEL), BF16),
                        pltpu.VMEM((LANE_TILES, tm, LANES), F32)],
        compiler_params=_params(),
        name="attn_proj",
    )(x2, norm_g, w_in, qg, kg, *next_weights)


def _attend_classes(q_ref, k_ref, v_ref, bias_ref, o_ref, stat_ref, kprev_ref, vprev_ref,
                    first, *, ncls, nq):
    lane = lax.broadcasted_iota(jnp.int32, (BLOCK, LANES), 1)
    low = lane < HEAD_DIM
    lane_row = lax.broadcasted_iota(jnp.int32, (1, LANES), 1)
    low_b = (lane_row < HEAD_DIM).astype(F32).astype(BF16)
    high_b = (lane_row >= HEAD_DIM).astype(F32).astype(BF16)
    ones = jnp.ones((2 * BLOCK, LANES), BF16)
    no_prev = jnp.zeros((BLOCK, LANES), BF16)

    for ci in range(ncls):
        for qb in range(nq):
            rows = slice(qb * BLOCK, (qb + 1) * BLOCK)
            stat_tile = jnp.zeros((BLOCK, LANES), F32)
            for pr in range(N_PAIRS):
                cols = slice(pr * LANES, (pr + 1) * LANES)
                q = q_ref[ci, rows, cols]
                q2 = jnp.concatenate([q * low_b, q * high_b], axis=0)
                if qb == 0:
                    kp = no_prev if kprev_ref is None else kprev_ref[ci, :, cols]
                    vp = no_prev if vprev_ref is None else vprev_ref[ci, :, cols]
                    kcat = jnp.concatenate([kp, k_ref[ci, 0:BLOCK, cols]], axis=0)
                    vcat = jnp.concatenate([vp, v_ref[ci, 0:BLOCK, cols]], axis=0)
                    bidx = first
                else:
                    krows = slice((qb - 1) * BLOCK, (qb + 1) * BLOCK)
                    kcat = k_ref[ci, krows, cols]
                    vcat = v_ref[ci, krows, cols]
                    bidx = 0
                s = lax.dot_general(q2, kcat, (((1,), (1,)), ((), ())),
                                    preferred_element_type=F32) + bias_ref[bidx, pr]
                m = jnp.max(s, axis=1, keepdims=True)
                p = jnp.exp2(s - m).astype(BF16)
                oe = jnp.dot(p, jnp.concatenate([vcat, ones], axis=1),
                             preferred_element_type=F32)
                o = jnp.where(low, oe[0:BLOCK, :LANES], oe[BLOCK:2 * BLOCK, :LANES])
                o_ref[ci, rows, cols] = o.astype(BF16)
                l = oe[:, LANES:]
                for hh in range(HEADS_PER_TILE):
                    head = HEADS_PER_TILE * pr + hh
                    hrows = slice(hh * BLOCK, (hh + 1) * BLOCK)
                    stat_tile = jnp.where(lane == head, m[hrows], stat_tile)
                    stat_tile = jnp.where(lane == N_HEADS + head, l[hrows], stat_tile)
            stat_ref[ci, rows, :] = stat_tile


def _attn_group_kernel(q_ref, k_ref, v_ref, bias_ref, o_ref, stat_ref, *scratch,
                       ncls, nq, carry):
    if not carry:
        _attend_classes(q_ref, k_ref, v_ref, bias_ref, o_ref, stat_ref, None, None, 1,
                        ncls=ncls, nq=nq)
        return
    kprev_ref, vprev_ref = scratch
    n = pl.program_id(2)

    @pl.when(n == 0)
    def _():
        kprev_ref[...] = jnp.zeros_like(kprev_ref)
        vprev_ref[...] = jnp.zeros_like(vprev_ref)

    _attend_classes(q_ref, k_ref, v_ref, bias_ref, o_ref, stat_ref, kprev_ref, vprev_ref,
                    jnp.where(n == 0, 1, 0), ncls=ncls, nq=nq)
    kprev_ref[...] = k_ref[:, (nq - 1) * BLOCK:nq * BLOCK, :]
    vprev_ref[...] = v_ref[:, (nq - 1) * BLOCK:nq * BLOCK, :]


def _group_bias(rel_bias, g):
    window, dilation = DILATED_GROUPS[g]
    span = window // dilation
    a = jnp.arange(BLOCK)[:, None]
    kk = jnp.arange(2 * BLOCK)[None, :]
    step = BLOCK + a - kk
    valid = (step >= 0) & (step <= span)
    dist = jnp.clip(step, 0, span) * dilation
    max_exact = N_BUCKETS // 2
    d = jnp.maximum(dist, 1).astype(F32)
    large = max_exact + (jnp.log(d / max_exact) / math.log(MAX_DISTANCE / max_exact)
                         * (N_BUCKETS - max_exact)).astype(jnp.int32)
    large = jnp.minimum(large, N_BUCKETS - 1)
    bucket = jnp.where(dist < max_exact, dist, large)
    tab = rel_bias[:, g * N_HEADS:(g + 1) * N_HEADS].astype(F32)
    onehot = (bucket[..., None] == jnp.arange(N_BUCKETS)).astype(F32)
    bias = jnp.einsum("akn,nh->hak", onehot, tab, precision=lax.Precision.HIGHEST) * LOG2E
    interior = jnp.where(valid[None], bias, MASKED)
    first = jnp.where((valid & (kk >= BLOCK))[None], bias, MASKED)
    both = jnp.stack([interior, first])
    return both.reshape(2, N_PAIRS, HEADS_PER_TILE * BLOCK, 2 * BLOCK)


def _attn_group(qkv, bias, g):
    batch, dilation, sub_len, _ = qkv.shape
    nq = min(ATTN_BLOCKS_PER_STEP, sub_len // BLOCK)
    rows = nq * BLOCK
    carry = rows < sub_len
    ncls = 1 if carry else min(dilation, ATTN_BLOCKS_PER_STEP // nq)

    def spec(width, col):
        return pl.BlockSpec((None, ncls, rows, width), lambda b, c, n: (b, c, n, col))

    scratch = [pltpu.VMEM((ncls, BLOCK, ATTN_WIDTH), BF16)] * 2 if carry else []
    return pl.pallas_call(
        functools.partial(_attn_group_kernel, ncls=ncls, nq=nq, carry=carry),
        grid=(batch, dilation // ncls, sub_len // rows),
        in_specs=[spec(ATTN_WIDTH, 0), spec(ATTN_WIDTH, 1), spec(ATTN_WIDTH, 2),
                  _resident(bias.shape)],
        out_specs=[spec(ATTN_WIDTH, 0), spec(LANES, 0)],
        out_shape=[jax.ShapeDtypeStruct((batch, dilation, sub_len, ATTN_WIDTH), BF16),
                   jax.ShapeDtypeStruct((batch, dilation, sub_len, LANES), F32)],
        scratch_shapes=scratch,
        compiler_params=_params(3),
        name=f"attn_group{g}",
    )(qkv, qkv, qkv, bias)


def _merge_out_kernel(*refs, tm, chunk, tiles_per_seq, n_fused):
    n_late = N_GROUPS - n_fused
    x_ref = refs[0]
    qkvb = [refs[1 + 4 * g:5 + 4 * g] for g in range(n_fused)]
    pos = 1 + 4 * n_fused
    late_o = refs[pos:pos + n_late]
    late_s = refs[pos + n_late:pos + 2 * n_late]
    gate_ref, expand_ref, w_ref, out_ref = refs[pos + 2 * n_late:pos + 2 * n_late + 4]
    scratch = refs[pos + 2 * n_late + 4:]
    carries = [scratch[4 * g:4 * g + 2] for g in range(n_fused)]
    fused_o = [scratch[4 * g + 2] for g in range(n_fused)]
    fused_s = [scratch[4 * g + 3] for g in range(n_fused)]
    snat_ref, onat_ref, stage_ref, y_ref = scratch[4 * n_fused:]
    o_refs = tuple(fused_o) + tuple(late_o)
    s_refs = tuple(fused_s) + tuple(late_s)
    assert DILATED_GROUPS[0][1] == 1 and chunk == tm

    seq_start = pl.program_id(0) % tiles_per_seq == 0

    @pl.when(seq_start)
    def _():
        for kprev_ref, vprev_ref in carries:
            kprev_ref[...] = jnp.zeros_like(kprev_ref)
            vprev_ref[...] = jnp.zeros_like(vprev_ref)

    for g in range(n_fused):
        q_ref, k_ref, v_ref, bias_ref = qkvb[g]
        r = DILATED_GROUPS[g][1]
        nq = tm // r // BLOCK
        _attend_classes(q_ref, k_ref, v_ref, bias_ref, fused_o[g], fused_s[g], *carries[g],
                        jnp.where(seq_start, 1, 0), ncls=r, nq=nq)
        carries[g][0][...] = k_ref[:, (nq - 1) * BLOCK:nq * BLOCK, :]
        carries[g][1][...] = v_ref[:, (nq - 1) * BLOCK:nq * BLOCK, :]

    head_lane = lax.broadcasted_iota(jnp.int32, (chunk, LANES), 1) < N_HEADS
    tiles_per_dot = MXU_COLS // LANES

    for ck in range(tm // chunk):
        row0 = ck * chunk
        nat = slice(row0, row0 + chunk)

        def to_natural(dst, piece, r):
            sub = chunk // r
            src = slice(row0 // r, row0 // r + sub)
            if r <= FAST_STRIDE:
                for c in range(r):
                    dst[pl.ds(row0 + c, sub, stride=r), :] = piece(c, src)
                return
            outer = r // FAST_STRIDE
            for c0 in range(FAST_STRIDE):
                for c1 in range(outer):
                    stage_ref[ck, c0, pl.ds(c1, sub, stride=outer), :] = (
                        piece(c1 * FAST_STRIDE + c0, src))
                dst[pl.ds(row0 + c0, sub * outer, stride=FAST_STRIDE), :] = stage_ref[ck, c0]

        for g in range(1, N_GROUPS):
            r = DILATED_GROUPS[g][1]
            to_natural(snat_ref.at[g - 1], lambda c, src: s_refs[g][c, src, :], r)
            for j in range(LANE_TILES):
                to_natural(onat_ref.at[g - 1, j],
                           lambda c, src: o_refs[g][c, src, j * LANES:(j + 1) * LANES].astype(F32),
                           r)

        def o_tile(g, j):
            if g == 0:
                return o_refs[0][0, nat, j * LANES:(j + 1) * LANES].astype(F32)
            return onat_ref[g - 1, j, nat, :]

        stats = [s_refs[0][0, nat, :]] + [snat_ref[g - 1, nat, :] for g in range(1, N_GROUPS)]
        m = jnp.maximum(jnp.maximum(stats[0], stats[1]), stats[2])
        es = [jnp.exp2(st - m) for st in stats]
        den = sum(e * pltpu.roll(st, LANES - N_HEADS, 1) for e, st in zip(es, stats))
        inv = 1.0 / jnp.where(head_lane, den, 1.0)
        weights = []
        for e in es:
            alpha = jnp.where(head_lane, e * inv, 0.0)
            hi = alpha.astype(BF16)
            lo = (alpha - hi.astype(F32)).astype(BF16)
            weights.append(jnp.concatenate([hi, lo], axis=1))
        for jj in range(LANE_TILES // tiles_per_dot):
            spread = [jnp.dot(w, expand_ref[:, jj * MXU_COLS:(jj + 1) * MXU_COLS],
                              preferred_element_type=F32) for w in weights]
            for jt in range(tiles_per_dot):
                j = jj * tiles_per_dot + jt
                cols = slice(j * LANES, (j + 1) * LANES)
                o = sum(spread[g][:, jt * LANES:(jt + 1) * LANES] * o_tile(g, j)
                        for g in range(N_GROUPS))
                y_ref[nat, cols] = (o * gate_ref[nat, cols].astype(F32)).astype(BF16)
        out_ref[nat, :] = x_ref[nat, :] + jnp.dot(y_ref[nat, :], w_ref[...],
                                                  preferred_element_type=F32)


def _merge_out(x2, fused, outs, stats, gate, w_out, seq):
    t = x2.shape[0]
    tm = TOKEN_TILE
    tiles_per_seq = seq // tm
    n_fused = len(fused)
    lane_head = jnp.arange(2 * LANES) % LANES
    expand = (lane_head[:, None] == (jnp.arange(ATTN_WIDTH) // HEAD_DIM)[None, :]).astype(BF16)
    row = lambda w: pl.BlockSpec((tm, w), lambda i: (i, 0))
    fused_specs, fused_args, fused_scratch = [], [], []
    for (qkv, bias), (_, r) in zip(fused, DILATED_GROUPS):
        sub = tm // r
        assert sub % BLOCK == 0
        for part in range(3):
            fused_specs.append(pl.BlockSpec(
                (None, r, sub, ATTN_WIDTH),
                lambda i, part=part: (i // tiles_per_seq, 0, i % tiles_per_seq, part)))
        fused_specs.append(_resident(bias.shape))
        fused_args += [qkv, qkv, qkv, bias]
        fused_scratch += [pltpu.VMEM((r, BLOCK, ATTN_WIDTH), BF16),
                          pltpu.VMEM((r, BLOCK, ATTN_WIDTH), BF16),
                          pltpu.VMEM((r, sub, ATTN_WIDTH), BF16),
                          pltpu.VMEM((r, sub, LANES), F32)]
    late = DILATED_GROUPS[n_fused:]
    o_specs = [_dilated_block(r, tm // r, ATTN_WIDTH, tiles_per_seq) for _, r in late]
    l_specs = [_dilated_block(r, tm // r, LANES, tiles_per_seq) for _, r in late]
    return pl.pallas_call(
        functools.partial(_merge_out_kernel, tm=tm, chunk=tm, tiles_per_seq=tiles_per_seq,
                          n_fused=n_fused),
        grid=(t // tm,),
        in_specs=[row(D_MODEL)] + fused_specs + o_specs + l_specs
                 + [row(ATTN_WIDTH), _resident(expand.shape),
                    _resident((ATTN_WIDTH, D_MODEL))],
        out_specs=row(D_MODEL),
        out_shape=jax.ShapeDtypeStruct((t, D_MODEL), F32),
        scratch_shapes=fused_scratch + [
            pltpu.VMEM((N_GROUPS - 1, tm, LANES), F32),
            pltpu.VMEM((N_GROUPS - 1, LANE_TILES, tm, LANES), F32),
            pltpu.VMEM((1, FAST_STRIDE, tm // FAST_STRIDE, LANES), F32),
            pltpu.VMEM((tm, ATTN_WIDTH), BF16)],
        compiler_params=_params(),
        name="merge_out",
    )(x2, *fused_args, *outs, *stats, gate, expand, w_out)


def _attn_layer(x2, norm_g, w_in, q_gain, k_gain, w_out, rel_bias, layer, batch, seq,
                next_weights):
    qkv0, qkv1, qkv2, gate, *cast = _attn_proj(x2, norm_g, w_in, q_gain, k_gain, layer,
                                               batch, seq, next_weights)
    qkvs = (qkv0, qkv1, qkv2)
    biases = [_group_bias(rel_bias, g) for g in range(N_GROUPS)]
    n_fused = sum(TOKEN_TILE // r >= BLOCK for _, r in DILATED_GROUPS)
    outs, stats = [], []
    for g in range(n_fused, N_GROUPS):
        o, st = _attn_group(qkvs[g], biases[g], g)
        outs.append(o)
        stats.append(st)
    fused = list(zip(qkvs[:n_fused], biases[:n_fused]))
    out = _merge_out(x2, fused, outs, stats, gate, w_out, seq)
    return out, cast


def kernel(x, conv_norm, conv_w_in, conv_w, conv_w_out, attn_norm, attn_w_in,
           attn_q_gain, attn_k_gain, attn_w_out, rel_bias):
    batch, seq, d = x.shape
    assert d == D_MODEL and seq % TOKEN_TILE == 0
    assert all(TOKEN_TILE % (BF16_SUBLANES * r) == 0 and seq % (BLOCK * r) == 0
               for _, r in DILATED_GROUPS)
    depth = conv_norm.shape[0] + attn_norm.shape[0]
    conv_norm3, attn_norm3 = conv_norm[:, None, :], attn_norm[:, None, :]
    w_in, w_out = conv_w_in[0].astype(BF16), conv_w_out[0].astype(BF16)
    x2 = x.reshape(batch * seq, d)
    for i in range(depth):
        j = i // 2
        if i % 2 == 0:
            nxt = (attn_w_in, attn_w_out) if i + 1 < depth else ()
            x2, cast = _conv_layer(x2, conv_norm3, w_in, conv_w, w_out, j, seq, nxt)
        else:
            nxt = (conv_w_in, conv_w_out) if i + 1 < depth else ()
            x2, cast = _attn_layer(x2, attn_norm3, w_in, attn_q_gain, attn_k_gain, w_out,
                                   rel_bias, j, batch, seq, nxt)
        if cast:
            w_in, w_out = cast
    return x2.reshape(batch, seq, d)
```

```python
import functools
import math

import jax
import jax.numpy as jnp
from jax import lax
from jax.experimental import pallas as pl
from jax.experimental.pallas import tpu as pltpu

D_MODEL = 1024
CONV_WIDTH = 2048
CONV_K = 3
HEAD_DIM = 64
N_HEADS = 16
ATTN_WIDTH = N_HEADS * HEAD_DIM
DILATED_GROUPS = ((128, 1), (512, 4), (2048, 16))
N_GROUPS = len(DILATED_GROUPS)
QKV_COLS = N_GROUPS * 3 * ATTN_WIDTH
BLOCK = 128
N_BUCKETS = 32
MAX_DISTANCE = 2048
EPS = 1e-6
MASKED = -1e30

LANES = 128
F32_SUBLANES = 8
BF16_SUBLANES = 16
MXU_COLS = 256
HEADS_PER_TILE = LANES // HEAD_DIM
N_PAIRS = N_HEADS // HEADS_PER_TILE
LANE_TILES = ATTN_WIDTH // LANES
VMEM_LIMIT = 56 * 1024 * 1024
TOKEN_TILE = 512
MERGE_ATTENDS = 1
ATTN_BLOCKS_PER_STEP = 16
FAST_STRIDE = 4
LOG2E = math.log2(math.e)

F32 = jnp.float32
BF16 = jnp.bfloat16


def _resident(shape, layer=None):
    if layer is None:
        return pl.BlockSpec(shape, lambda *_: (0,) * len(shape),
                            pipeline_mode=pl.Buffered(1))
    return pl.BlockSpec((None,) + tuple(shape), lambda *_: (layer,) + (0,) * len(shape),
                        pipeline_mode=pl.Buffered(1))


def _cast_rider_specs(weights, layer, n_steps):
    in_specs, out_specs, out_shapes = [], [], []
    for w in weights:
        _, rows, cols = w.shape
        slab = rows // n_steps
        assert slab * n_steps == rows and slab % BF16_SUBLANES == 0
        in_specs.append(pl.BlockSpec((None, slab, cols), lambda i: (layer, i, 0)))
        out_specs.append(pl.BlockSpec((slab, cols), lambda i: (i, 0)))
        out_shapes.append(jax.ShapeDtypeStruct((rows, cols), BF16))
    return in_specs, out_specs, out_shapes


def _params(n_axes=1):
    return pltpu.CompilerParams(
        dimension_semantics=("arbitrary",) * n_axes, vmem_limit_bytes=VMEM_LIMIT)


def _rms_rows(x, gain):
    ms = jnp.mean(x * x, axis=-1, keepdims=True)
    return x * lax.rsqrt(ms + EPS) * gain


def _conv_layer_kernel(*refs, tm, te, tiles_per_seq, n_riders):
    x_ref, g_ref, win_ref, wc_ref, wout_ref = refs[:5]
    rider_in = refs[5:5 + n_riders]
    o_ref = refs[5 + n_riders]
    rider_out = refs[6 + n_riders:6 + 2 * n_riders]
    h_ref, carry_ref = refs[6 + 2 * n_riders:]
    for src, dst in zip(rider_in, rider_out):
        dst[...] = src[...].astype(BF16)
    i = pl.program_id(0)
    x = x_ref[...]
    h_ref[...] = _rms_rows(x, g_ref[...]).astype(BF16)

    @pl.when(i % tiles_per_seq == 0)
    def _():
        carry_ref[...] = jnp.zeros_like(carry_ref)

    rows = lax.broadcasted_iota(jnp.int32, (tm, te), 0)
    for j in range(CONV_WIDTH // te):
        cols = slice(j * te, (j + 1) * te)

        def proj(p):
            w = win_ref[:, p * CONV_WIDTH + j * te:p * CONV_WIDTH + (j + 1) * te]
            return jnp.dot(h_ref[...], w, preferred_element_type=F32)

        v = proj(1) * proj(2)
        prev1 = carry_ref[1:2, cols]
        prev2 = carry_ref[0:1, cols]
        v1 = jnp.where(rows == 0, prev1, pltpu.roll(v, 1, 0))
        v2 = jnp.where(rows == 0, prev2,
                       jnp.where(rows == 1, prev1, pltpu.roll(v, 2, 0)))
        carry_ref[0:2, cols] = v[tm - 2:tm]
        wc = wc_ref[:, cols]
        conv = wc[0:1] * v2 + wc[1:2] * v1 + wc[2:3] * v
        y = proj(0) * conv * jax.nn.silu(proj(3))
        part = jnp.dot(y.astype(BF16), wout_ref[cols, :],
                       preferred_element_type=F32)
        if j == 0:
            o_ref[...] = x + part
        else:
            o_ref[...] += part


def _conv_layer(x2, norm_g, w_in, w_conv, w_out, layer, seq, next_weights):
    t = x2.shape[0]
    tm, te = 2 * TOKEN_TILE, MXU_COLS
    r_in, r_out, r_shapes = _cast_rider_specs(next_weights, layer, t // tm)
    kern = functools.partial(_conv_layer_kernel, tm=tm, te=te,
                             tiles_per_seq=seq // tm, n_riders=len(next_weights))
    out, *cast = pl.pallas_call(
        kern,
        grid=(t // tm,),
        in_specs=[
            pl.BlockSpec((tm, D_MODEL), lambda i: (i, 0)),
            _resident((1, D_MODEL), layer),
            _resident((D_MODEL, 4 * CONV_WIDTH)),
            _resident((CONV_K, CONV_WIDTH), layer),
            _resident((CONV_WIDTH, D_MODEL)),
        ] + r_in,
        out_specs=[pl.BlockSpec((tm, D_MODEL), lambda i: (i, 0))] + r_out,
        out_shape=[jax.ShapeDtypeStruct((t, D_MODEL), F32)] + r_shapes,
        scratch_shapes=[pltpu.VMEM((tm, D_MODEL), BF16),
                        pltpu.VMEM((F32_SUBLANES, CONV_WIDTH), F32)],
        compiler_params=_params(),
        name="conv_layer",
    )(x2, norm_g, w_in, w_conv, w_out, *next_weights)
    return out, cast


def _attn_proj_kernel(*refs, tm, n_riders):
    x_ref, g_ref, w_ref, qg_ref, kg_ref = refs[:5]
    rider_in = refs[5:5 + n_riders]
    qkv0_ref, qkv1_ref, qkv2_ref, gate_ref = refs[5 + n_riders:9 + n_riders]
    rider_out = refs[9 + n_riders:9 + 2 * n_riders]
    h_ref, slab_ref = refs[9 + 2 * n_riders:]
    for src, dst in zip(rider_in, rider_out):
        dst[...] = src[...].astype(BF16)
    h32 = _rms_rows(x_ref[...], g_ref[...])
    h_ref[0] = h32.astype(BF16)
    for j in range(LANE_TILES):
        slab_ref[j] = h32[:, j * LANES:(j + 1) * LANES]
    for g in range(1, N_GROUPS):
        r = DILATED_GROUPS[g][1]
        sub = tm // r
        for c in range(r):
            for j in range(LANE_TILES):
                piece = slab_ref[j, pl.ds(c, sub, stride=r), :]
                h_ref[g, c * sub:(c + 1) * sub, j * LANES:(j + 1) * LANES] = piece.astype(BF16)

    z = jnp.dot(h_ref[0], w_ref[:, QKV_COLS:QKV_COLS + ATTN_WIDTH],
                preferred_element_type=F32)
    gate_ref[...] = jax.nn.silu(z).astype(BF16)

    out_refs = (qkv0_ref, qkv1_ref, qkv2_ref)
    low = lax.broadcasted_iota(jnp.int32, (tm, LANES), 1) < HEAD_DIM
    for g in range(N_GROUPS):
        r = DILATED_GROUPS[g][1]
        sub = tm // r
        for part in range(3):
            col0 = (g * 3 + part) * ATTN_WIDTH
            a = jnp.dot(h_ref[g], w_ref[:, col0:col0 + ATTN_WIDTH],
                        preferred_element_type=F32)
            gain_ref = (qg_ref, kg_ref, None)[part]
            for s in range(LANE_TILES):
                sl = slice(s * LANES, (s + 1) * LANES)
                a_s = a[:, sl]
                if gain_ref is not None:
                    sq = a_s * a_s
                    ss_low = jnp.sum(jnp.where(low, sq, 0.0), axis=-1, keepdims=True)
                    ss_high = jnp.sum(jnp.where(low, 0.0, sq), axis=-1, keepdims=True)
                    ss = jnp.where(low, ss_low, ss_high)
                    a_s = a_s * lax.rsqrt(ss * (1.0 / HEAD_DIM) + EPS) * gain_ref[g:g + 1, sl]
                a_s = a_s.astype(BF16)
                for c in range(r):
                    out_refs[g][c, :, part * ATTN_WIDTH + s * LANES:
                                part * ATTN_WIDTH + (s + 1) * LANES] = a_s[c * sub:(c + 1) * sub]


def _dilated_block(r, sub, width, tiles_per_seq):
    return pl.BlockSpec((None, r, sub, width),
                        lambda i: (i // tiles_per_seq, 0, i % tiles_per_seq, 0))


def _attn_proj(x2, norm_g, w_in, q_gain, k_gain, layer, batch, seq, next_weights):
    t = x2.shape[0]
    tm = TOKEN_TILE
    tiles_per_seq = seq // tm
    qg = jnp.tile(q_gain, (1, 1, N_HEADS)) * (HEAD_DIM ** -0.5 * LOG2E)
    kg = jnp.tile(k_gain, (1, 1, N_HEADS))
    qkv_specs, qkv_shapes = [], []
    for _, r in DILATED_GROUPS:
        qkv_specs.append(_dilated_block(r, tm // r, 3 * ATTN_WIDTH, tiles_per_seq))
        qkv_shapes.append(jax.ShapeDtypeStruct((batch, r, seq // r, 3 * ATTN_WIDTH), BF16))
    r_in, r_out, r_shapes = _cast_rider_specs(next_weights, layer + 1, t // tm)
    return pl.pallas_call(
        functools.partial(_attn_proj_kernel, tm=tm, n_riders=len(next_weights)),
        grid=(t // tm,),
        in_specs=[
            pl.BlockSpec((tm, D_MODEL), lambda i: (i, 0)),
            _resident((1, D_MODEL), layer),
            _resident((D_MODEL, QKV_COLS + ATTN_WIDTH)),
            _resident((N_GROUPS, ATTN_WIDTH), layer),
            _resident((N_GROUPS, ATTN_WIDTH), layer),
        ] + r_in,
        out_specs=qkv_specs + [pl.BlockSpec((tm, ATTN_WIDTH), lambda i: (i, 0))] + r_out,
        out_shape=qkv_shapes + [jax.ShapeDtypeStruct((t, ATTN_WIDTH), BF16)] + r_shapes,
        scratch_shapes=[pltpu.VMEM((N_GROUPS, tm, D_MODEL), BF16),
                        pltpu.VMEM((LANE_TILES, tm, LANES), F32)],
        compiler_params=_params(),
        name="attn_proj",
    )(x2, norm_g, w_in, qg, kg, *next_weights)


def _attend_classes(q_ref, k_ref, v_ref, bias_ref, o_ref, stat_ref, kprev_ref, vprev_ref,
                    first, *, ncls, nq):
    lane = lax.broadcasted_iota(jnp.int32, (BLOCK, LANES), 1)
    low = lane < HEAD_DIM
    lane_row = lax.broadcasted_iota(jnp.int32, (1, LANES), 1)
    low_b = (lane_row < HEAD_DIM).astype(F32).astype(BF16)
    high_b = (lane_row >= HEAD_DIM).astype(F32).astype(BF16)
    ones = jnp.ones((2 * BLOCK, LANES), BF16)
    no_prev = jnp.zeros((BLOCK, LANES), BF16)

    for ci in range(ncls):
        for qb in range(nq):
            rows = slice(qb * BLOCK, (qb + 1) * BLOCK)
            stat_tile = jnp.zeros((BLOCK, LANES), F32)
            for pr in range(N_PAIRS):
                cols = slice(pr * LANES, (pr + 1) * LANES)
                q = q_ref[ci, rows, cols]
                q2 = jnp.concatenate([q * low_b, q * high_b], axis=0)
                if qb == 0:
                    kp = no_prev if kprev_ref is None else kprev_ref[ci, :, cols]
                    vp = no_prev if vprev_ref is None else vprev_ref[ci, :, cols]
                    kcat = jnp.concatenate([kp, k_ref[ci, 0:BLOCK, cols]], axis=0)
                    vcat = jnp.concatenate([vp, v_ref[ci, 0:BLOCK, cols]], axis=0)
                    bidx = first
                else:
                    krows = slice((qb - 1) * BLOCK, (qb + 1) * BLOCK)
                    kcat = k_ref[ci, krows, cols]
                    vcat = v_ref[ci, krows, cols]
                    bidx = 0
                s = lax.dot_general(q2, kcat, (((1,), (1,)), ((), ())),
                                    preferred_element_type=F32) + bias_ref[bidx, pr]
                m = jnp.max(s, axis=1, keepdims=True)
                p = jnp.exp2(s - m).astype(BF16)
                oe = jnp.dot(p, jnp.concatenate([vcat, ones], axis=1),
                             preferred_element_type=F32)
                o = jnp.where(low, oe[0:BLOCK, :LANES], oe[BLOCK:2 * BLOCK, :LANES])
                o_ref[ci, rows, cols] = o.astype(BF16)
                l = oe[:, LANES:]
                for hh in range(HEADS_PER_TILE):
                    head = HEADS_PER_TILE * pr + hh
                    hrows = slice(hh * BLOCK, (hh + 1) * BLOCK)
                    stat_tile = jnp.where(lane == head, m[hrows], stat_tile)
                    stat_tile = jnp.where(lane == N_HEADS + head, l[hrows], stat_tile)
            stat_ref[ci, rows, :] = stat_tile


def _attn_group_kernel(q_ref, k_ref, v_ref, bias_ref, o_ref, stat_ref, *scratch,
                       ncls, nq, carry):
    if not carry:
        _attend_classes(q_ref, k_ref, v_ref, bias_ref, o_ref, stat_ref, None, None, 1,
                        ncls=ncls, nq=nq)
        return
    kprev_ref, vprev_ref = scratch
    n = pl.program_id(2)

    @pl.when(n == 0)
    def _():
        kprev_ref[...] = jnp.zeros_like(kprev_ref)
        vprev_ref[...] = jnp.zeros_like(vprev_ref)

    _attend_classes(q_ref, k_ref, v_ref, bias_ref, o_ref, stat_ref, kprev_ref, vprev_ref,
                    jnp.where(n == 0, 1, 0), ncls=ncls, nq=nq)
    kprev_ref[...] = k_ref[:, (nq - 1) * BLOCK:nq * BLOCK, :]
    vprev_ref[...] = v_ref[:, (nq - 1) * BLOCK:nq * BLOCK, :]


def _group_bias(rel_bias, g):
    window, dilation = DILATED_GROUPS[g]
    span = window // dilation
    a = jnp.arange(BLOCK)[:, None]
    kk = jnp.arange(2 * BLOCK)[None, :]
    step = BLOCK + a - kk
    valid = (step >= 0) & (step <= span)
    dist = jnp.clip(step, 0, span) * dilation
    max_exact = N_BUCKETS // 2
    d = jnp.maximum(dist, 1).astype(F32)
    large = max_exact + (jnp.log(d / max_exact) / math.log(MAX_DISTANCE / max_exact)
                         * (N_BUCKETS - max_exact)).astype(jnp.int32)
    large = jnp.minimum(large, N_BUCKETS - 1)
    bucket = jnp.where(dist < max_exact, dist, large)
    tab = rel_bias[:, g * N_HEADS:(g + 1) * N_HEADS].astype(F32)
    onehot = (bucket[..., None] == jnp.arange(N_BUCKETS)).astype(F32)
    bias = jnp.einsum("akn,nh->hak", onehot, tab, precision=lax.Precision.HIGHEST) * LOG2E
    interior = jnp.where(valid[None], bias, MASKED)
    first = jnp.where((valid & (kk >= BLOCK))[None], bias, MASKED)
    both = jnp.stack([interior, first])
    return both.reshape(2, N_PAIRS, HEADS_PER_TILE * BLOCK, 2 * BLOCK)


def _attn_group(qkv, bias, g):
    batch, dilation, sub_len, _ = qkv.shape
    nq = min(ATTN_BLOCKS_PER_STEP, sub_len // BLOCK)
    rows = nq * BLOCK
    carry = rows < sub_len
    ncls = 1 if carry else min(dilation, ATTN_BLOCKS_PER_STEP // nq)

    def spec(width, col):
        return pl.BlockSpec((None, ncls, rows, width), lambda b, c, n: (b, c, n, col))

    scratch = [pltpu.VMEM((ncls, BLOCK, ATTN_WIDTH), BF16)] * 2 if carry else []
    return pl.pallas_call(
        functools.partial(_attn_group_kernel, ncls=ncls, nq=nq, carry=carry),
        grid=(batch, dilation // ncls, sub_len // rows),
        in_specs=[spec(ATTN_WIDTH, 0), spec(ATTN_WIDTH, 1), spec(ATTN_WIDTH, 2),
                  _resident(bias.shape)],
        out_specs=[spec(ATTN_WIDTH, 0), spec(LANES, 0)],
        out_shape=[jax.ShapeDtypeStruct((batch, dilation, sub_len, ATTN_WIDTH), BF16),
                   jax.ShapeDtypeStruct((batch, dilation, sub_len, LANES), F32)],
        scratch_shapes=scratch,
        compiler_params=_params(3),
        name=f"attn_group{g}",
    )(qkv, qkv, qkv, bias)


def _merge_out_kernel(*refs, tm, chunk, tiles_per_seq, n_fused):
    n_late = N_GROUPS - n_fused
    x_ref = refs[0]
    qkvb = [refs[1 + 4 * g:5 + 4 * g] for g in range(n_fused)]
    pos = 1 + 4 * n_fused
    late_o = refs[pos:pos + n_late]
    late_s = refs[pos + n_late:pos + 2 * n_late]
    gate_ref, expand_ref, w_ref, out_ref = refs[pos + 2 * n_late:pos + 2 * n_late + 4]
    scratch = refs[pos + 2 * n_late + 4:]
    carries = [scratch[4 * g:4 * g + 2] for g in range(n_fused)]
    fused_o = [scratch[4 * g + 2] for g in range(n_fused)]
    fused_s = [scratch[4 * g + 3] for g in range(n_fused)]
    snat_ref, onat_ref, stage_ref, y_ref = scratch[4 * n_fused:]
    o_refs = tuple(fused_o) + tuple(late_o)
    s_refs = tuple(fused_s) + tuple(late_s)
    assert DILATED_GROUPS[0][1] == 1 and chunk == tm

    seq_start = pl.program_id(0) % tiles_per_seq == 0

    @pl.when(seq_start)
    def _():
        for kprev_ref, vprev_ref in carries:
            kprev_ref[...] = jnp.zeros_like(kprev_ref)
            vprev_ref[...] = jnp.zeros_like(vprev_ref)

    for g in range(n_fused):
        q_ref, k_ref, v_ref, bias_ref = qkvb[g]
        r = DILATED_GROUPS[g][1]
        nq = tm // r // BLOCK
        _attend_classes(q_ref, k_ref, v_ref, bias_ref, fused_o[g], fused_s[g], *carries[g],
                        jnp.where(seq_start, 1, 0), ncls=r, nq=nq)
        carries[g][0][...] = k_ref[:, (nq - 1) * BLOCK:nq * BLOCK, :]
        carries[g][1][...] = v_ref[:, (nq - 1) * BLOCK:nq * BLOCK, :]

    head_lane = lax.broadcasted_iota(jnp.int32, (chunk, LANES), 1) < N_HEADS
    tiles_per_dot = MXU_COLS // LANES

    for ck in range(tm // chunk):
        row0 = ck * chunk
        nat = slice(row0, row0 + chunk)

        def to_natural(dst, piece, r):
            sub = chunk // r
            src = slice(row0 // r, row0 // r + sub)
            if r <= FAST_STRIDE:
                for c in range(r):
                    dst[pl.ds(row0 + c, sub, stride=r), :] = piece(c, src)
                return
            outer = r // FAST_STRIDE
            for c0 in range(FAST_STRIDE):
                for c1 in range(outer):
                    stage_ref[ck, c0, pl.ds(c1, sub, stride=outer), :] = (
                        piece(c1 * FAST_STRIDE + c0, src))
                dst[pl.ds(row0 + c0, sub * outer, stride=FAST_STRIDE), :] = stage_ref[ck, c0]

        for g in range(1, N_GROUPS):
            r = DILATED_GROUPS[g][1]
            to_natural(snat_ref.at[g - 1], lambda c, src: s_refs[g][c, src, :], r)
            for j in range(LANE_TILES):
                to_natural(onat_ref.at[g - 1, j],
                           lambda c, src: o_refs[g][c, src, j * LANES:(j + 1) * LANES].astype(F32),
                           r)

        def o_tile(g, j):
            if g == 0:
                return o_refs[0][0, nat, j * LANES:(j + 1) * LANES].astype(F32)
            return onat_ref[g - 1, j, nat, :]

        stats = [s_refs[0][0, nat, :]] + [snat_ref[g - 1, nat, :] for g in range(1, N_GROUPS)]
        m = jnp.maximum(jnp.maximum(stats[0], stats[1]), stats[2])
        es = [jnp.exp2(st - m) for st in stats]
        den = sum(e * pltpu.roll(st, LANES - N_HEADS, 1) for e, st in zip(es, stats))
        inv = 1.0 / jnp.where(head_lane, den, 1.0)
        weights = []
        for e in es:
            alpha = jnp.where(head_lane, e * inv, 0.0)
            hi = alpha.astype(BF16)
            lo = (alpha - hi.astype(F32)).astype(BF16)
            weights.append(jnp.concatenate([hi, lo], axis=1))
        for jj in range(LANE_TILES // tiles_per_dot):
            spread = [jnp.dot(w, expand_ref[:, jj * MXU_COLS:(jj + 1) * MXU_COLS],
                              preferred_element_type=F32) for w in weights]
            for jt in range(tiles_per_dot):
                j = jj * tiles_per_dot + jt
                cols = slice(j * LANES, (j + 1) * LANES)
                o = sum(spread[g][:, jt * LANES:(jt + 1) * LANES] * o_tile(g, j)
                        for g in range(N_GROUPS))
                y_ref[nat, cols] = (o * gate_ref[nat, cols].astype(F32)).astype(BF16)
        out_ref[nat, :] = x_ref[nat, :] + jnp.dot(y_ref[nat, :], w_ref[...],
                                                  preferred_element_type=F32)


def _merge_out(x2, fused, outs, stats, gate, w_out, seq):
    t = x2.shape[0]
    tm = TOKEN_TILE
    tiles_per_seq = seq // tm
    n_fused = len(fused)
    lane_head = jnp.arange(2 * LANES) % LANES
    expand = (lane_head[:, None] == (jnp.arange(ATTN_WIDTH) // HEAD_DIM)[None, :]).astype(BF16)
    row = lambda w: pl.BlockSpec((tm, w), lambda i: (i, 0))
    fused_specs, fused_args, fused_scratch = [], [], []
    for (qkv, bias), (_, r) in zip(fused, DILATED_GROUPS):
        sub = tm // r
        assert sub % BLOCK == 0
        for part in range(3):
            fused_specs.append(pl.BlockSpec(
                (None, r, sub, ATTN_WIDTH),
                lambda i, part=part: (i // tiles_per_seq, 0, i % tiles_per_seq, part)))
        fused_specs.append(_resident(bias.shape))
        fused_args += [qkv, qkv, qkv, bias]
        fused_scratch += [pltpu.VMEM((r, BLOCK, ATTN_WIDTH), BF16),
                          pltpu.VMEM((r, BLOCK, ATTN_WIDTH), BF16),
                          pltpu.VMEM((r, sub, ATTN_WIDTH), BF16),
                          pltpu.VMEM((r, sub, LANES), F32)]
    late = DILATED_GROUPS[n_fused:]
    o_specs = [_dilated_block(r, tm // r, ATTN_WIDTH, tiles_per_seq) for _, r in late]
    l_specs = [_dilated_block(r, tm // r, LANES, tiles_per_seq) for _, r in late]
    return pl.pallas_call(
        functools.partial(_merge_out_kernel, tm=tm, chunk=tm, tiles_per_seq=tiles_per_seq,
                          n_fused=n_fused),
        grid=(t // tm,),
        in_specs=[row(D_MODEL)] + fused_specs + o_specs + l_specs
                 + [row(ATTN_WIDTH), _resident(expand.shape),
                    _resident((ATTN_WIDTH, D_MODEL))],
        out_specs=row(D_MODEL),
        out_shape=jax.ShapeDtypeStruct((t, D_MODEL), F32),
        scratch_shapes=fused_scratch + [
            pltpu.VMEM((N_GROUPS - 1, tm, LANES), F32),
            pltpu.VMEM((N_GROUPS - 1, LANE_TILES, tm, LANES), F32),
            pltpu.VMEM((1, FAST_STRIDE, tm // FAST_STRIDE, LANES), F32),
            pltpu.VMEM((tm, ATTN_WIDTH), BF16)],
        compiler_params=_params(),
        name="merge_out",
    )(x2, *fused_args, *outs, *stats, gate, expand, w_out)


def _attn_layer(x2, norm_g, w_in, q_gain, k_gain, w_out, rel_bias, layer, batch, seq,
                next_weights):
    qkv0, qkv1, qkv2, gate, *cast = _attn_proj(x2, norm_g, w_in, q_gain, k_gain, layer,
                                               batch, seq, next_weights)
    qkvs = (qkv0, qkv1, qkv2)
    biases = [_group_bias(rel_bias, g) for g in range(N_GROUPS)]
    n_fused = MERGE_ATTENDS
    assert all(TOKEN_TILE // r >= BLOCK for _, r in DILATED_GROUPS[:n_fused])
    outs, stats = [], []
    for g in range(n_fused, N_GROUPS):
        o, st = _attn_group(qkvs[g], biases[g], g)
        outs.append(o)
        stats.append(st)
    fused = list(zip(qkvs[:n_fused], biases[:n_fused]))
    out = _merge_out(x2, fused, outs, stats, gate, w_out, seq)
    return out, cast


def kernel(x, conv_norm, conv_w_in, conv_w, conv_w_out, attn_norm, attn_w_in,
           attn_q_gain, attn_k_gain, attn_w_out, rel_bias):
    batch, seq, d = x.shape
    assert d == D_MODEL and seq % TOKEN_TILE == 0
    assert all(TOKEN_TILE % (BF16_SUBLANES * r) == 0 and seq % (BLOCK * r) == 0
               for _, r in DILATED_GROUPS)
    depth = conv_norm.shape[0] + attn_norm.shape[0]
    conv_norm3, attn_norm3 = conv_norm[:, None, :], attn_norm[:, None, :]
    w_in, w_out = conv_w_in[0].astype(BF16), conv_w_out[0].astype(BF16)
    x2 = x.reshape(batch * seq, d)
    for i in range(depth):
        j = i // 2
        if i % 2 == 0:
            nxt = (attn_w_in, attn_w_out) if i + 1 < depth else ()
            x2, cast = _conv_layer(x2, conv_norm3, w_in, conv_w, w_out, j, seq, nxt)
        else:
            nxt = (conv_w_in, conv_w_out) if i + 1 < depth else ()
            x2, cast = _attn_layer(x2, attn_norm3, w_in, attn_q_gain, attn_k_gain, w_out,
                                   rel_bias, j, batch, seq, nxt)
        if cast:
            w_in, w_out = cast
    return x2.reshape(batch, seq, d)
```

```python
import functools
import math

import jax
import jax.numpy as jnp
from jax import lax
from jax.experimental import pallas as pl
from jax.experimental.pallas import tpu as pltpu

D_MODEL = 1024
CONV_WIDTH = 2048
CONV_K = 3
HEAD_DIM = 64
N_HEADS = 16
ATTN_WIDTH = N_HEADS * HEAD_DIM
DILATED_GROUPS = ((128, 1), (512, 4), (2048, 16))
N_GROUPS = len(DILATED_GROUPS)
QKV_COLS = N_GROUPS * 3 * ATTN_WIDTH
BLOCK = 128
N_BUCKETS = 32
MAX_DISTANCE = 2048
EPS = 1e-6
MASKED = -1e30

LANES = 128
F32_SUBLANES = 8
BF16_SUBLANES = 16
MXU_COLS = 256
HEADS_PER_TILE = LANES // HEAD_DIM
N_PAIRS = N_HEADS // HEADS_PER_TILE
LANE_TILES = ATTN_WIDTH // LANES
VMEM_LIMIT = 56 * 1024 * 1024
TOKEN_TILE = 512
MERGE_ATTENDS = 1
ATTN_BLOCKS_PER_STEP = 16
FAST_STRIDE = 4
LOG2E = math.log2(math.e)

F32 = jnp.float32
BF16 = jnp.bfloat16


def _resident(shape, layer=None):
    if layer is None:
        return pl.BlockSpec(shape, lambda *_: (0,) * len(shape),
                            pipeline_mode=pl.Buffered(1))
    return pl.BlockSpec((None,) + tuple(shape), lambda *_: (layer,) + (0,) * len(shape),
                        pipeline_mode=pl.Buffered(1))


def _cast_rider_specs(weights, layer, n_steps):
    in_specs, out_specs, out_shapes = [], [], []
    for w in weights:
        _, rows, cols = w.shape
        slab = rows // n_steps
        assert slab * n_steps == rows and slab % BF16_SUBLANES == 0
        in_specs.append(pl.BlockSpec((None, slab, cols), lambda i: (layer, i, 0)))
        out_specs.append(pl.BlockSpec((slab, cols), lambda i: (i, 0)))
        out_shapes.append(jax.ShapeDtypeStruct((rows, cols), BF16))
    return in_specs, out_specs, out_shapes


def _params(n_axes=1, fusable_inputs=None):
    return pltpu.CompilerParams(
        dimension_semantics=("arbitrary",) * n_axes, vmem_limit_bytes=VMEM_LIMIT,
        allow_input_fusion=fusable_inputs)


def _rms_rows(x, gain):
    ms = jnp.mean(x * x, axis=-1, keepdims=True)
    return x * lax.rsqrt(ms + EPS) * gain


def _conv_layer_kernel(*refs, tm, te, tiles_per_seq, n_riders):
    x_ref, g_ref, win_ref, wc_ref, wout_ref = refs[:5]
    rider_in = refs[5:5 + n_riders]
    o_ref = refs[5 + n_riders]
    rider_out = refs[6 + n_riders:6 + 2 * n_riders]
    h_ref, carry_ref = refs[6 + 2 * n_riders:]
    for src, dst in zip(rider_in, rider_out):
        dst[...] = src[...].astype(BF16)
    i = pl.program_id(0)
    x = x_ref[...]
    h_ref[...] = _rms_rows(x, g_ref[...]).astype(BF16)

    @pl.when(i % tiles_per_seq == 0)
    def _():
        carry_ref[...] = jnp.zeros_like(carry_ref)

    rows = lax.broadcasted_iota(jnp.int32, (tm, te), 0)
    for j in range(CONV_WIDTH // te):
        cols = slice(j * te, (j + 1) * te)

        def proj(p):
            w = win_ref[:, p * CONV_WIDTH + j * te:p * CONV_WIDTH + (j + 1) * te]
            return jnp.dot(h_ref[...], w, preferred_element_type=F32)

        v = proj(1) * proj(2)
        prev1 = carry_ref[1:2, cols]
        prev2 = carry_ref[0:1, cols]
        v1 = jnp.where(rows == 0, prev1, pltpu.roll(v, 1, 0))
        v2 = jnp.where(rows == 0, prev2,
                       jnp.where(rows == 1, prev1, pltpu.roll(v, 2, 0)))
        carry_ref[0:2, cols] = v[tm - 2:tm]
        wc = wc_ref[:, cols]
        conv = wc[0:1] * v2 + wc[1:2] * v1 + wc[2:3] * v
        y = proj(0) * conv * jax.nn.silu(proj(3))
        part = jnp.dot(y.astype(BF16), wout_ref[cols, :],
                       preferred_element_type=F32)
        if j == 0:
            o_ref[...] = x + part
        else:
            o_ref[...] += part


def _conv_layer(x2, norm_g, w_in, w_conv, w_out, layer, seq, next_weights):
    t = x2.shape[0]
    tm, te = 2 * TOKEN_TILE, MXU_COLS
    r_in, r_out, r_shapes = _cast_rider_specs(next_weights, layer, t // tm)
    kern = functools.partial(_conv_layer_kernel, tm=tm, te=te,
                             tiles_per_seq=seq // tm, n_riders=len(next_weights))
    out, *cast = pl.pallas_call(
        kern,
        grid=(t // tm,),
        in_specs=[
            pl.BlockSpec((tm, D_MODEL), lambda i: (i, 0)),
            _resident((1, D_MODEL), layer),
            _resident((D_MODEL, 4 * CONV_WIDTH)),
            _resident((CONV_K, CONV_WIDTH), layer),
            _resident((CONV_WIDTH, D_MODEL)),
        ] + r_in,
        out_specs=[pl.BlockSpec((tm, D_MODEL), lambda i: (i, 0))] + r_out,
        out_shape=[jax.ShapeDtypeStruct((t, D_MODEL), F32)] + r_shapes,
        scratch_shapes=[pltpu.VMEM((tm, D_MODEL), BF16),
                        pltpu.VMEM((F32_SUBLANES, CONV_WIDTH), F32)],
        compiler_params=_params(fusable_inputs=[False, False, True, False, True]
                                + [False] * len(next_weights)),
        name="conv_layer",
    )(x2, norm_g, w_in, w_conv, w_out, *next_weights)
    return out, cast


def _attn_proj_kernel(*refs, tm, n_riders):
    x_ref, g_ref, w_ref, qg_ref, kg_ref = refs[:5]
    rider_in = refs[5:5 + n_riders]
    qkv0_ref, qkv1_ref, qkv2_ref, gate_ref = refs[5 + n_riders:9 + n_riders]
    rider_out = refs[9 + n_riders:9 + 2 * n_riders]
    h_ref, slab_ref = refs[9 + 2 * n_riders:]
    for src, dst in zip(rider_in, rider_out):
        dst[...] = src[...].astype(BF16)
    h32 = _rms_rows(x_ref[...], g_ref[...])
    h_ref[0] = h32.astype(BF16)
    for j in range(LANE_TILES):
        slab_ref[j] = h32[:, j * LANES:(j + 1) * LANES]
    for g in range(1, N_GROUPS):
        r = DILATED_GROUPS[g][1]
        sub = tm // r
        for c in range(r):
            for j in range(LANE_TILES):
                piece = slab_ref[j, pl.ds(c, sub, stride=r), :]
                h_ref[g, c * sub:(c + 1) * sub, j * LANES:(j + 1) * LANES] = piece.astype(BF16)

    z = jnp.dot(h_ref[0], w_ref[:, QKV_COLS:QKV_COLS + ATTN_WIDTH],
                preferred_element_type=F32)
    gate_ref[...] = jax.nn.silu(z).astype(BF16)

    out_refs = (qkv0_ref, qkv1_ref, qkv2_ref)
    low = lax.broadcasted_iota(jnp.int32, (tm, LANES), 1) < HEAD_DIM
    for g in range(N_GROUPS):
        r = DILATED_GROUPS[g][1]
        sub = tm // r
        for part in range(3):
            col0 = (g * 3 + part) * ATTN_WIDTH
            a = jnp.dot(h_ref[g], w_ref[:, col0:col0 + ATTN_WIDTH],
                        preferred_element_type=F32)
            gain_ref = (qg_ref, kg_ref, None)[part]
            for s in range(LANE_TILES):
                sl = slice(s * LANES, (s + 1) * LANES)
                a_s = a[:, sl]
                if gain_ref is not None:
                    sq = a_s * a_s
                    ss_low = jnp.sum(jnp.where(low, sq, 0.0), axis=-1, keepdims=True)
                    ss_high = jnp.sum(jnp.where(low, 0.0, sq), axis=-1, keepdims=True)
                    ss = jnp.where(low, ss_low, ss_high)
                    a_s = a_s * lax.rsqrt(ss * (1.0 / HEAD_DIM) + EPS) * gain_ref[g:g + 1, sl]
                a_s = a_s.astype(BF16)
                for c in range(r):
                    out_refs[g][c, :, part * ATTN_WIDTH + s * LANES:
                                part * ATTN_WIDTH + (s + 1) * LANES] = a_s[c * sub:(c + 1) * sub]


def _dilated_block(r, sub, width, tiles_per_seq):
    return pl.BlockSpec((None, r, sub, width),
                        lambda i: (i // tiles_per_seq, 0, i % tiles_per_seq, 0))


def _attn_proj(x2, norm_g, w_in, q_gain, k_gain, layer, batch, seq, next_weights):
    t = x2.shape[0]
    tm = TOKEN_TILE
    tiles_per_seq = seq // tm
    qg = jnp.tile(q_gain, (1, 1, N_HEADS)) * (HEAD_DIM ** -0.5 * LOG2E)
    kg = jnp.tile(k_gain, (1, 1, N_HEADS))
    qkv_specs, qkv_shapes = [], []
    for _, r in DILATED_GROUPS:
        qkv_specs.append(_dilated_block(r, tm // r, 3 * ATTN_WIDTH, tiles_per_seq))
        qkv_shapes.append(jax.ShapeDtypeStruct((batch, r, seq // r, 3 * ATTN_WIDTH), BF16))
    r_in, r_out, r_shapes = _cast_rider_specs(next_weights, layer + 1, t // tm)
    return pl.pallas_call(
        functools.partial(_attn_proj_kernel, tm=tm, n_riders=len(next_weights)),
        grid=(t // tm,),
        in_specs=[
            pl.BlockSpec((tm, D_MODEL), lambda i: (i, 0)),
            _resident((1, D_MODEL), layer),
            _resident((D_MODEL, QKV_COLS + ATTN_WIDTH)),
            _resident((N_GROUPS, ATTN_WIDTH), layer),
            _resident((N_GROUPS, ATTN_WIDTH), layer),
        ] + r_in,
        out_specs=qkv_specs + [pl.BlockSpec((tm, ATTN_WIDTH), lambda i: (i, 0))] + r_out,
        out_shape=qkv_shapes + [jax.ShapeDtypeStruct((t, ATTN_WIDTH), BF16)] + r_shapes,
        scratch_shapes=[pltpu.VMEM((N_GROUPS, tm, D_MODEL), BF16),
                        pltpu.VMEM((LANE_TILES, tm, LANES), F32)],
        compiler_params=_params(),
        name="attn_proj",
    )(x2, norm_g, w_in, qg, kg, *next_weights)


def _attend_classes(q_ref, k_ref, v_ref, bias_ref, o_ref, stat_ref, kprev_ref, vprev_ref,
                    first, *, ncls, nq):
    lane = lax.broadcasted_iota(jnp.int32, (BLOCK, LANES), 1)
    low = lane < HEAD_DIM
    lane_row = lax.broadcasted_iota(jnp.int32, (1, LANES), 1)
    low_b = (lane_row < HEAD_DIM).astype(F32).astype(BF16)
    high_b = (lane_row >= HEAD_DIM).astype(F32).astype(BF16)
    ones = jnp.ones((2 * BLOCK, LANES), BF16)
    no_prev = jnp.zeros((BLOCK, LANES), BF16)

    for ci in range(ncls):
        for qb in range(nq):
            rows = slice(qb * BLOCK, (qb + 1) * BLOCK)
            stat_tile = jnp.zeros((BLOCK, LANES), F32)
            for pr in range(N_PAIRS):
                cols = slice(pr * LANES, (pr + 1) * LANES)
                q = q_ref[ci, rows, cols]
                q2 = jnp.concatenate([q * low_b, q * high_b], axis=0)
                if qb == 0:
                    kp = no_prev if kprev_ref is None else kprev_ref[ci, :, cols]
                    vp = no_prev if vprev_ref is None else vprev_ref[ci, :, cols]
                    kcat = jnp.concatenate([kp, k_ref[ci, 0:BLOCK, cols]], axis=0)
                    vcat = jnp.concatenate([vp, v_ref[ci, 0:BLOCK, cols]], axis=0)
                    bidx = first
                else:
                    krows = slice((qb - 1) * BLOCK, (qb + 1) * BLOCK)
                    kcat = k_ref[ci, krows, cols]
                    vcat = v_ref[ci, krows, cols]
                    bidx = 0
                s = lax.dot_general(q2, kcat, (((1,), (1,)), ((), ())),
                                    preferred_element_type=F32) + bias_ref[bidx, pr]
                m = jnp.max(s, axis=1, keepdims=True)
                p = jnp.exp2(s - m).astype(BF16)
                oe = jnp.dot(p, jnp.concatenate([vcat, ones], axis=1),
                             preferred_element_type=F32)
                o = jnp.where(low, oe[0:BLOCK, :LANES], oe[BLOCK:2 * BLOCK, :LANES])
                o_ref[ci, rows, cols] = o.astype(BF16)
                l = oe[:, LANES:]
                for hh in range(HEADS_PER_TILE):
                    head = HEADS_PER_TILE * pr + hh
                    hrows = slice(hh * BLOCK, (hh + 1) * BLOCK)
                    stat_tile = jnp.where(lane == head, m[hrows], stat_tile)
                    stat_tile = jnp.where(lane == N_HEADS + head, l[hrows], stat_tile)
            stat_ref[ci, rows, :] = stat_tile


def _attn_group_kernel(q_ref, k_ref, v_ref, bias_ref, o_ref, stat_ref, *scratch,
                       ncls, nq, carry):
    if not carry:
        _attend_classes(q_ref, k_ref, v_ref, bias_ref, o_ref, stat_ref, None, None, 1,
                        ncls=ncls, nq=nq)
        return
    kprev_ref, vprev_ref = scratch
    n = pl.program_id(2)

    @pl.when(n == 0)
    def _():
        kprev_ref[...] = jnp.zeros_like(kprev_ref)
        vprev_ref[...] = jnp.zeros_like(vprev_ref)

    _attend_classes(q_ref, k_ref, v_ref, bias_ref, o_ref, stat_ref, kprev_ref, vprev_ref,
                    jnp.where(n == 0, 1, 0), ncls=ncls, nq=nq)
    kprev_ref[...] = k_ref[:, (nq - 1) * BLOCK:nq * BLOCK, :]
    vprev_ref[...] = v_ref[:, (nq - 1) * BLOCK:nq * BLOCK, :]


def _group_bias(rel_bias, g):
    window, dilation = DILATED_GROUPS[g]
    span = window // dilation
    a = jnp.arange(BLOCK)[:, None]
    kk = jnp.arange(2 * BLOCK)[None, :]
    step = BLOCK + a - kk
    valid = (step >= 0) & (step <= span)
    dist = jnp.clip(step, 0, span) * dilation
    max_exact = N_BUCKETS // 2
    d = jnp.maximum(dist, 1).astype(F32)
    large = max_exact + (jnp.log(d / max_exact) / math.log(MAX_DISTANCE / max_exact)
                         * (N_BUCKETS - max_exact)).astype(jnp.int32)
    large = jnp.minimum(large, N_BUCKETS - 1)
    bucket = jnp.where(dist < max_exact, dist, large)
    tab = rel_bias[:, g * N_HEADS:(g + 1) * N_HEADS].astype(F32)
    onehot = (bucket[..., None] == jnp.arange(N_BUCKETS)).astype(F32)
    bias = jnp.einsum("akn,nh->hak", onehot, tab, precision=lax.Precision.HIGHEST) * LOG2E
    interior = jnp.where(valid[None], bias, MASKED)
    first = jnp.where((valid & (kk >= BLOCK))[None], bias, MASKED)
    both = jnp.stack([interior, first])
    return both.reshape(2, N_PAIRS, HEADS_PER_TILE * BLOCK, 2 * BLOCK)


def _attn_group(qkv, bias, g):
    batch, dilation, sub_len, _ = qkv.shape
    nq = min(ATTN_BLOCKS_PER_STEP, sub_len // BLOCK)
    rows = nq * BLOCK
    carry = rows < sub_len
    ncls = 1 if carry else min(dilation, ATTN_BLOCKS_PER_STEP // nq)

    def spec(width, col):
        return pl.BlockSpec((None, ncls, rows, width), lambda b, c, n: (b, c, n, col))

    scratch = [pltpu.VMEM((ncls, BLOCK, ATTN_WIDTH), BF16)] * 2 if carry else []
    return pl.pallas_call(
        functools.partial(_attn_group_kernel, ncls=ncls, nq=nq, carry=carry),
        grid=(batch, dilation // ncls, sub_len // rows),
        in_specs=[spec(ATTN_WIDTH, 0), spec(ATTN_WIDTH, 1), spec(ATTN_WIDTH, 2),
                  _resident(bias.shape)],
        out_specs=[spec(ATTN_WIDTH, 0), spec(LANES, 0)],
        out_shape=[jax.ShapeDtypeStruct((batch, dilation, sub_len, ATTN_WIDTH), BF16),
                   jax.ShapeDtypeStruct((batch, dilation, sub_len, LANES), F32)],
        scratch_shapes=scratch,
        compiler_params=_params(3),
        name=f"attn_group{g}",
    )(qkv, qkv, qkv, bias)


def _merge_out_kernel(*refs, tm, chunk, tiles_per_seq, n_fused):
    n_late = N_GROUPS - n_fused
    x_ref = refs[0]
    qkvb = [refs[1 + 4 * g:5 + 4 * g] for g in range(n_fused)]
    pos = 1 + 4 * n_fused
    late_o = refs[pos:pos + n_late]
    late_s = refs[pos + n_late:pos + 2 * n_late]
    gate_ref, expand_ref, w_ref, out_ref = refs[pos + 2 * n_late:pos + 2 * n_late + 4]
    scratch = refs[pos + 2 * n_late + 4:]
    carries = [scratch[4 * g:4 * g + 2] for g in range(n_fused)]
    fused_o = [scratch[4 * g + 2] for g in range(n_fused)]
    fused_s = [scratch[4 * g + 3] for g in range(n_fused)]
    snat_ref, onat_ref, stage_ref, y_ref = scratch[4 * n_fused:]
    o_refs = tuple(fused_o) + tuple(late_o)
    s_refs = tuple(fused_s) + tuple(late_s)
    assert DILATED_GROUPS[0][1] == 1 and chunk == tm

    seq_start = pl.program_id(0) % tiles_per_seq == 0

    @pl.when(seq_start)
    def _():
        for kprev_ref, vprev_ref in carries:
            kprev_ref[...] = jnp.zeros_like(kprev_ref)
            vprev_ref[...] = jnp.zeros_like(vprev_ref)

    for g in range(n_fused):
        q_ref, k_ref, v_ref, bias_ref = qkvb[g]
        r = DILATED_GROUPS[g][1]
        nq = tm // r // BLOCK
        _attend_classes(q_ref, k_ref, v_ref, bias_ref, fused_o[g], fused_s[g], *carries[g],
                        jnp.where(seq_start, 1, 0), ncls=r, nq=nq)
        carries[g][0][...] = k_ref[:, (nq - 1) * BLOCK:nq * BLOCK, :]
        carries[g][1][...] = v_ref[:, (nq - 1) * BLOCK:nq * BLOCK, :]

    head_lane = lax.broadcasted_iota(jnp.int32, (chunk, LANES), 1) < N_HEADS
    tiles_per_dot = MXU_COLS // LANES

    for ck in range(tm // chunk):
        row0 = ck * chunk
        nat = slice(row0, row0 + chunk)

        def to_natural(dst, piece, r):
            sub = chunk // r
            src = slice(row0 // r, row0 // r + sub)
            if r <= FAST_STRIDE:
                for c in range(r):
                    dst[pl.ds(row0 + c, sub, stride=r), :] = piece(c, src)
                return
            outer = r // FAST_STRIDE
            for c0 in range(FAST_STRIDE):
                for c1 in range(outer):
                    stage_ref[ck, c0, pl.ds(c1, sub, stride=outer), :] = (
                        piece(c1 * FAST_STRIDE + c0, src))
                dst[pl.ds(row0 + c0, sub * outer, stride=FAST_STRIDE), :] = stage_ref[ck, c0]

        for g in range(1, N_GROUPS):
            r = DILATED_GROUPS[g][1]
            to_natural(snat_ref.at[g - 1], lambda c, src: s_refs[g][c, src, :], r)
            for j in range(LANE_TILES):
                to_natural(onat_ref.at[g - 1, j],
                           lambda c, src: o_refs[g][c, src, j * LANES:(j + 1) * LANES].astype(F32),
                           r)

        def o_tile(g, j):
            if g == 0:
                return o_refs[0][0, nat, j * LANES:(j + 1) * LANES].astype(F32)
            return onat_ref[g - 1, j, nat, :]

        stats = [s_refs[0][0, nat, :]] + [snat_ref[g - 1, nat, :] for g in range(1, N_GROUPS)]
        m = jnp.maximum(jnp.maximum(stats[0], stats[1]), stats[2])
        es = [jnp.exp2(st - m) for st in stats]
        den = sum(e * pltpu.roll(st, LANES - N_HEADS, 1) for e, st in zip(es, stats))
        inv = 1.0 / jnp.where(head_lane, den, 1.0)
        weights = []
        for e in es:
            alpha = jnp.where(head_lane, e * inv, 0.0)
            hi = alpha.astype(BF16)
            lo = (alpha - hi.astype(F32)).astype(BF16)
            weights.append(jnp.concatenate([hi, lo], axis=1))
        for jj in range(LANE_TILES // tiles_per_dot):
            spread = [jnp.dot(w, expand_ref[:, jj * MXU_COLS:(jj + 1) * MXU_COLS],
                              preferred_element_type=F32) for w in weights]
            for jt in range(tiles_per_dot):
                j = jj * tiles_per_dot + jt
                cols = slice(j * LANES, (j + 1) * LANES)
                o = sum(spread[g][:, jt * LANES:(jt + 1) * LANES] * o_tile(g, j)
                        for g in range(N_GROUPS))
                y_ref[nat, cols] = (o * gate_ref[nat, cols].astype(F32)).astype(BF16)
        out_ref[nat, :] = x_ref[nat, :] + jnp.dot(y_ref[nat, :], w_ref[...],
                                                  preferred_element_type=F32)


def _merge_out(x2, fused, outs, stats, gate, w_out, seq):
    t = x2.shape[0]
    tm = TOKEN_TILE
    tiles_per_seq = seq // tm
    n_fused = len(fused)
    lane_head = jnp.arange(2 * LANES) % LANES
    expand = (lane_head[:, None] == (jnp.arange(ATTN_WIDTH) // HEAD_DIM)[None, :]).astype(BF16)
    row = lambda w: pl.BlockSpec((tm, w), lambda i: (i, 0))
    fused_specs, fused_args, fused_scratch = [], [], []
    for (qkv, bias), (_, r) in zip(fused, DILATED_GROUPS):
        sub = tm // r
        assert sub % BLOCK == 0
        for part in range(3):
            fused_specs.append(pl.BlockSpec(
                (None, r, sub, ATTN_WIDTH),
                lambda i, part=part: (i // tiles_per_seq, 0, i % tiles_per_seq, part)))
        fused_specs.append(_resident(bias.shape))
        fused_args += [qkv, qkv, qkv, bias]
        fused_scratch += [pltpu.VMEM((r, BLOCK, ATTN_WIDTH), BF16),
                          pltpu.VMEM((r, BLOCK, ATTN_WIDTH), BF16),
                          pltpu.VMEM((r, sub, ATTN_WIDTH), BF16),
                          pltpu.VMEM((r, sub, LANES), F32)]
    late = DILATED_GROUPS[n_fused:]
    o_specs = [_dilated_block(r, tm // r, ATTN_WIDTH, tiles_per_seq) for _, r in late]
    l_specs = [_dilated_block(r, tm // r, LANES, tiles_per_seq) for _, r in late]
    return pl.pallas_call(
        functools.partial(_merge_out_kernel, tm=tm, chunk=tm, tiles_per_seq=tiles_per_seq,
                          n_fused=n_fused),
        grid=(t // tm,),
        in_specs=[row(D_MODEL)] + fused_specs + o_specs + l_specs
                 + [row(ATTN_WIDTH), _resident(expand.shape),
                    _resident((ATTN_WIDTH, D_MODEL))],
        out_specs=row(D_MODEL),
        out_shape=jax.ShapeDtypeStruct((t, D_MODEL), F32),
        scratch_shapes=fused_scratch + [
            pltpu.VMEM((N_GROUPS - 1, tm, LANES), F32),
            pltpu.VMEM((N_GROUPS - 1, LANE_TILES, tm, LANES), F32),
            pltpu.VMEM((1, FAST_STRIDE, tm // FAST_STRIDE, LANES), F32),
            pltpu.VMEM((tm, ATTN_WIDTH), BF16)],
        compiler_params=_params(),
        name="merge_out",
    )(x2, *fused_args, *outs, *stats, gate, expand, w_out)


def _attn_layer(x2, norm_g, w_in, q_gain, k_gain, w_out, rel_bias, layer, batch, seq,
                next_weights):
    qkv0, qkv1, qkv2, gate, *cast = _attn_proj(x2, norm_g, w_in, q_gain, k_gain, layer,
                                               batch, seq, next_weights)
    qkvs = (qkv0, qkv1, qkv2)
    biases = [_group_bias(rel_bias, g) for g in range(N_GROUPS)]
    n_fused = MERGE_ATTENDS
    assert all(TOKEN_TILE // r >= BLOCK for _, r in DILATED_GROUPS[:n_fused])
    outs, stats = [], []
    for g in range(n_fused, N_GROUPS):
        o, st = _attn_group(qkvs[g], biases[g], g)
        outs.append(o)
        stats.append(st)
    fused = list(zip(qkvs[:n_fused], biases[:n_fused]))
    out = _merge_out(x2, fused, outs, stats, gate, w_out, seq)
    return out, cast


def kernel(x, conv_norm, conv_w_in, conv_w, conv_w_out, attn_norm, attn_w_in,
           attn_q_gain, attn_k_gain, attn_w_out, rel_bias):
    batch, seq, d = x.shape
    assert d == D_MODEL and seq % TOKEN_TILE == 0
    assert all(TOKEN_TILE % (BF16_SUBLANES * r) == 0 and seq % (BLOCK * r) == 0
               for _, r in DILATED_GROUPS)
    depth = conv_norm.shape[0] + attn_norm.shape[0]
    conv_norm3, attn_norm3 = conv_norm[:, None, :], attn_norm[:, None, :]
    w_in, w_out = conv_w_in[0].astype(BF16), conv_w_out[0].astype(BF16)
    x2 = x.reshape(batch * seq, d)
    for i in range(depth):
        j = i // 2
        if i % 2 == 0:
            nxt = (attn_w_in, attn_w_out) if i + 1 < depth else ()
            x2, cast = _conv_layer(x2, conv_norm3, w_in, conv_w, w_out, j, seq, nxt)
        else:
            nxt = (conv_w_in, conv_w_out) if i + 1 < depth else ()
            x2, cast = _attn_layer(x2, attn_norm3, w_in, attn_q_gain, attn_k_gain, w_out,
                                   rel_bias, j, batch, seq, nxt)
        if cast:
            w_in, w_out = cast
    return x2.reshape(batch, seq, d)
```
